```python
import math
import jax, jax.numpy as jnp
from jax import lax
import numpy as np

D_MODEL = 2048
BATCH = 4
SEQ = 2048
DEPTH = 1
DEC_BATCH = 128
DEC_SEQ = 8
PAST_LEN = 16384
PAGE_SIZE = 128

MIX_WIDTH = D_MODEL
LRU_WIDTH = MIX_WIDTH // 2
LRU_BLOCKS = 8
LRU_BLOCK = LRU_WIDTH // LRU_BLOCKS
CONV_WIDTH = 4
LRU_C = 8.0
RWKV_WIDTH = MIX_WIDTH - LRU_WIDTH
RWKV_HEAD = 64
RWKV_HEADS = RWKV_WIDTH // RWKV_HEAD
DECAY_LORA = 64
A_LORA = 64
G_LORA = 160
SHIFT_WIDTH = 3 * RWKV_WIDTH + DECAY_LORA + A_LORA + G_LORA
P_TOTAL = 2 * LRU_WIDTH + SHIFT_WIDTH
RWKV_SPLITS = [RWKV_WIDTH, 2 * RWKV_WIDTH, 3 * RWKV_WIDTH,
               3 * RWKV_WIDTH + DECAY_LORA, 3 * RWKV_WIDTH + DECAY_LORA + A_LORA]
DECAY_SCALE = math.exp(-0.5)
N_MEM = 256
X_HEADS = 4
X_HEAD_DIM = D_MODEL // X_HEADS
D_FF = 4 * D_MODEL
LN_EPS = 1e-5
GN_EPS = 64e-5
ALPHA = (2 * DEPTH) ** 0.25
BETA = (8 * DEPTH) ** -0.25

kernel_name = "hymba_rglru_rwkv7_memxattn_step"


def layer_norm(x, g, b):
    xf = x.astype(jnp.float32)
    mu = jnp.mean(xf, -1, keepdims=True)
    var = jnp.mean(jnp.square(xf - mu), -1, keepdims=True)
    return ((xf - mu) * lax.rsqrt(var + LN_EPS) * g + b).astype(x.dtype)


def causal_conv(u, buf, w, b):
    T = u.shape[1]
    up = jnp.concatenate([buf.astype(u.dtype), u], axis=1)
    out = b + w[0] * up[:, 0:T]
    for j in range(1, CONV_WIDTH):
        out = out + w[j] * up[:, j:j + T]
    return out, up[:, T:]


def rg_lru(xc, h0, wa, ba, wx, bx, lam):
    B, T, _ = xc.shape
    f32 = jnp.float32
    xf = xc.astype(f32)
    xb = xf.reshape(B, T, LRU_BLOCKS, LRU_BLOCK)
    r = jax.nn.sigmoid(jnp.einsum('btnc,ncd->btnd', xb, wa.astype(f32)).reshape(B, T, LRU_WIDTH) + ba)
    i = jax.nn.sigmoid(jnp.einsum('btnc,ncd->btnd', xb, wx.astype(f32)).reshape(B, T, LRU_WIDTH) + bx)
    log_a = -LRU_C * r * jax.nn.softplus(-lam.astype(f32))
    a = jnp.exp(log_a)
    bterm = jnp.sqrt(-jnp.expm1(2.0 * log_a)) * (i * xf)
    bterm = bterm.at[:, 0].add(a[:, 0] * h0.astype(f32))

    def combine(lhs, rhs):
        a1, b1 = lhs
        a2, b2 = rhs
        return a1 * a2, a2 * b1 + b2

    _, h = lax.associative_scan(combine, (a, bterm), axis=1)
    return h, h[:, -1]


def rwkv7_mix(u, shift0, S0, mu, w0, w_up, a0, a_up, g_up, k_k, k_a, r_k, ln_w, ln_b):
    B, T, _ = u.shape
    f32 = jnp.float32
    uf = u.astype(f32)
    prev = jnp.concatenate([shift0.astype(f32)[:, None], uf[:, :-1]], axis=1)
    z = uf + (prev - uf) * mu
    r, k, v, wd, ad, gd = jnp.split(z, RWKV_SPLITS, axis=-1)
    w = jnp.exp(-DECAY_SCALE * jax.nn.sigmoid(w0 + jnp.tanh(wd) @ w_up.astype(f32)))
    a = jax.nn.sigmoid(a0 + ad @ a_up.astype(f32))
    g = jax.nn.sigmoid(gd) @ g_up.astype(f32)

    def heads(t):
        return t.reshape(B, T, RWKV_HEADS, RWKV_HEAD)

    kk = heads(k * k_k)
    kk = kk * lax.rsqrt(jnp.maximum(jnp.sum(kk * kk, -1, keepdims=True), 1e-24))
    k = k * (1.0 + (a - 1.0) * k_a)
    r_h, k_h, v_h, a_h, w_h = heads(r), heads(k), heads(v), heads(a), heads(w)

    def step(S, inp):
        r_t, w_t, k_t, v_t, kk_t, a_t = inp
        s_kk = jnp.einsum('bhij,bhj->bhi', S, kk_t)
        S = (S * w_t[:, :, None, :]
             - s_kk[..., None] * (kk_t * a_t)[:, :, None, :]
             + v_t[..., None] * k_t[:, :, None, :])
        return S, jnp.einsum('bhij,bhj->bhi', S, r_t)

    seq = tuple(jnp.swapaxes(t, 0, 1) for t in (r_h, w_h, k_h, v_h, kk, a_h))
    S_T, o = lax.scan(step, S0.astype(f32), seq)
    o = jnp.swapaxes(o, 0, 1)
    mean = jnp.mean(o, -1, keepdims=True)
    var = jnp.mean(jnp.square(o - mean), -1, keepdims=True)
    o = (o - mean) * lax.rsqrt(var + GN_EPS) * ln_w.reshape(RWKV_HEADS, RWKV_HEAD) + ln_b.reshape(RWKV_HEADS, RWKV_HEAD)
    bonus = jnp.sum(r_h * k_h * r_k, -1, keepdims=True) * v_h
    y = (o + bonus).reshape(B, T, RWKV_WIDTH) * g
    return y, S_T, u[:, -1]


def hybrid_layer(x, mem_k, mem_v, h0, conv0, S0, shift0,
                 w_in, lru_conv_w, lru_conv_b, lru_wa, lru_ba, lru_wx, lru_bx, lru_L,
                 rwkv_mu, rwkv_w0, rwkv_w_up, rwkv_a0, rwkv_a_up, rwkv_g_up,
                 rwkv_k_k, rwkv_k_a, rwkv_r_k, rwkv_ln_w, rwkv_ln_b,
                 w_out, ln1_g, ln1_b, xa_wq, xa_wo, ln2_g, ln2_b,
                 mlp_w1, mlp_w2, ln3_g, ln3_b):
    B, T, _ = x.shape
    dt = x.dtype
    p = x @ w_in
    u_x = p[..., :LRU_WIDTH]
    u_gate = p[..., LRU_WIDTH:2 * LRU_WIDTH]
    u_rwkv = p[..., 2 * LRU_WIDTH:]
    xc, conv_new = causal_conv(u_x, conv0, lru_conv_w, lru_conv_b)
    h, h_last = rg_lru(xc, h0, lru_wa, lru_ba, lru_wx, lru_bx, lru_L)
    y_lru = h.astype(dt) * jax.nn.gelu(u_gate)
    y_rwkv, S_new, shift_new = rwkv7_mix(u_rwkv, shift0, S0, rwkv_mu, rwkv_w0, rwkv_w_up, rwkv_a0,
                                         rwkv_a_up, rwkv_g_up, rwkv_k_k, rwkv_k_a, rwkv_r_k,
                                         rwkv_ln_w, rwkv_ln_b)
    mix = jnp.concatenate([y_lru, y_rwkv.astype(dt)], axis=-1) @ w_out
    x = layer_norm(ALPHA * x + mix, ln1_g, ln1_b)
    q = (x @ xa_wq).reshape(B, T, X_HEADS, X_HEAD_DIM)
    s = jnp.einsum('bthd,bmhd->bhtm', q, mem_k.astype(dt)).astype(jnp.float32) * (X_HEAD_DIM ** -0.5)
    pr = jax.nn.softmax(s, axis=-1).astype(dt)
    att = jnp.einsum('bhtm,bmhd->bthd', pr, mem_v.astype(dt)).reshape(B, T, D_MODEL) @ xa_wo
    x = layer_norm(ALPHA * x + att, ln2_g, ln2_b)
    ff = jnp.square(jax.nn.relu(x @ mlp_w1)) @ mlp_w2
    x = layer_norm(ALPHA * x + ff, ln3_g, ln3_b)
    return x, h_last.astype(dt), conv_new.astype(dt), S_new.astype(dt), shift_new.astype(dt)


def setup_inputs(seed: int = 0) -> dict:
    key = jax.random.key(seed)
    ks = iter(jax.random.split(key, 64))
    f32 = jnp.float32

    def nrm(shape, s):
        return jax.random.normal(next(ks), shape, f32) * s

    def unif(shape, lo, hi):
        return jax.random.uniform(next(ks), shape, f32, minval=lo, maxval=hi)

    Lr = DEPTH
    u = unif((Lr, LRU_WIDTH), 0.9, 0.999)
    a_base = u ** (1.0 / LRU_C)
    lru_L = jnp.log(a_base) - jnp.log1p(-a_base)
    return {
        "x_prompt": nrm((BATCH, SEQ, D_MODEL), 1.0),
        "x_sample": nrm((DEC_BATCH, DEC_SEQ, D_MODEL), 1.0),
        "mem_prompt": nrm((BATCH, N_MEM, D_MODEL), 1.0),
        "cache_mem_k": nrm((Lr, DEC_BATCH, N_MEM, X_HEADS, X_HEAD_DIM), 1.0),
        "cache_mem_v": nrm((Lr, DEC_BATCH, N_MEM, X_HEADS, X_HEAD_DIM), BETA),
        "state_lru_h": nrm((Lr, DEC_BATCH, LRU_WIDTH), 0.5),
        "state_lru_conv": nrm((Lr, DEC_BATCH, CONV_WIDTH - 1, LRU_WIDTH), 1.0),
        "state_rwkv_S": nrm((Lr, DEC_BATCH, RWKV_HEADS, RWKV_HEAD, RWKV_HEAD), 0.3),
        "state_rwkv_shift": nrm((Lr, DEC_BATCH, SHIFT_WIDTH), 1.0),
        "w_in": nrm((Lr, D_MODEL, P_TOTAL), D_MODEL ** -0.5),
        "lru_conv_w": nrm((Lr, CONV_WIDTH, LRU_WIDTH), CONV_WIDTH ** -0.5),
        "lru_conv_b": nrm((Lr, LRU_WIDTH), 0.01),
        "lru_wa": nrm((Lr, LRU_BLOCKS, LRU_BLOCK, LRU_BLOCK), LRU_BLOCK ** -0.5),
        "lru_ba": nrm((Lr, LRU_WIDTH), 0.01),
        "lru_wx": nrm((Lr, LRU_BLOCKS, LRU_BLOCK, LRU_BLOCK), LRU_BLOCK ** -0.5),
        "lru_bx": nrm((Lr, LRU_WIDTH), 0.01),
        "lru_L": lru_L,
        "rwkv_mu": unif((Lr, SHIFT_WIDTH), 0.0, 1.0),
        "rwkv_w0": unif((Lr, RWKV_WIDTH), -6.0, 1.0),
        "rwkv_w_up": nrm((Lr, DECAY_LORA, RWKV_WIDTH), 0.5 * DECAY_LORA ** -0.5),
        "rwkv_a0": nrm((Lr, RWKV_WIDTH), 0.1),
        "rwkv_a_up": nrm((Lr, A_LORA, RWKV_WIDTH), 0.5 * A_LORA ** -0.5),
        "rwkv_g_up": nrm((Lr, G_LORA, RWKV_WIDTH), G_LORA ** -0.5),
        "rwkv_k_k": 0.85 + nrm((Lr, RWKV_WIDTH), 0.05),
        "rwkv_k_a": 1.0 + nrm((Lr, RWKV_WIDTH), 0.05),
        "rwkv_r_k": nrm((Lr, RWKV_HEADS, RWKV_HEAD), 0.1),
        "rwkv_ln_w": 1.0 + nrm((Lr, RWKV_WIDTH), 0.05),
        "rwkv_ln_b": nrm((Lr, RWKV_WIDTH), 0.02),
        "w_out": nrm((Lr, MIX_WIDTH, D_MODEL), MIX_WIDTH ** -0.5 * BETA),
        "ln1_g": 1.0 + nrm((Lr, D_MODEL), 0.02),
        "ln1_b": nrm((Lr, D_MODEL), 0.02),
        "xa_wq": nrm((Lr, D_MODEL, D_MODEL), D_MODEL ** -0.5),
        "xa_wk": nrm((Lr, D_MODEL, D_MODEL), D_MODEL ** -0.5),
        "xa_wv": nrm((Lr, D_MODEL, D_MODEL), D_MODEL ** -0.5 * BETA),
        "xa_wo": nrm((Lr, D_MODEL, D_MODEL), D_MODEL ** -0.5 * BETA),
        "ln2_g": 1.0 + nrm((Lr, D_MODEL), 0.02),
        "ln2_b": nrm((Lr, D_MODEL), 0.02),
        "mlp_w1": nrm((Lr, D_MODEL, D_FF), D_MODEL ** -0.5),
        "mlp_w2": nrm((Lr, D_FF, D_MODEL), D_FF ** -0.5 * BETA),
        "ln3_g": 1.0 + nrm((Lr, D_MODEL), 0.02),
        "ln3_b": nrm((Lr, D_MODEL), 0.02),
    }


def reference(x_prompt, x_sample, mem_prompt, cache_mem_k, cache_mem_v, state_lru_h, state_lru_conv,
              state_rwkv_S, state_rwkv_shift,
              w_in, lru_conv_w, lru_conv_b, lru_wa, lru_ba, lru_wx, lru_bx, lru_L,
              rwkv_mu, rwkv_w0, rwkv_w_up, rwkv_a0, rwkv_a_up, rwkv_g_up, rwkv_k_k, rwkv_k_a, rwkv_r_k,
              rwkv_ln_w, rwkv_ln_b,
              w_out, ln1_g, ln1_b, xa_wq, xa_wk, xa_wv, xa_wo, ln2_g, ln2_b,
              mlp_w1, mlp_w2, ln3_g, ln3_b):
    B = x_prompt.shape[0]
    dt = x_prompt.dtype
    yp, ys = x_prompt, x_sample
    mk_p, mv_p, h_p, c_p, S_p, sh_p = [], [], [], [], [], []
    h_s, c_s, S_s, sh_s = [], [], [], []
    for l in range(DEPTH):
        wl = [t[l] for t in (w_in, lru_conv_w, lru_conv_b, lru_wa, lru_ba, lru_wx, lru_bx, lru_L,
                             rwkv_mu, rwkv_w0, rwkv_w_up, rwkv_a0, rwkv_a_up, rwkv_g_up,
                             rwkv_k_k, rwkv_k_a, rwkv_r_k, rwkv_ln_w, rwkv_ln_b,
                             w_out, ln1_g, ln1_b, xa_wq, xa_wo, ln2_g, ln2_b,
                             mlp_w1, mlp_w2, ln3_g, ln3_b)]
        mem_k = (mem_prompt @ xa_wk[l]).reshape(B, N_MEM, X_HEADS, X_HEAD_DIM)
        mem_v = (mem_prompt @ xa_wv[l]).reshape(B, N_MEM, X_HEADS, X_HEAD_DIM)
        yp, hl, cl, Sl, shl = hybrid_layer(
            yp, mem_k, mem_v,
            jnp.zeros((B, LRU_WIDTH), dt), jnp.zeros((B, CONV_WIDTH - 1, LRU_WIDTH), dt),
            jnp.zeros((B, RWKV_HEADS, RWKV_HEAD, RWKV_HEAD), dt), jnp.zeros((B, SHIFT_WIDTH), dt),
            *wl)
        mk_p.append(mem_k); mv_p.append(mem_v); h_p.append(hl); c_p.append(cl); S_p.append(Sl); sh_p.append(shl)
        ys, hl, cl, Sl, shl = hybrid_layer(
            ys, cache_mem_k[l], cache_mem_v[l],
            state_lru_h[l], state_lru_conv[l], state_rwkv_S[l], state_rwkv_shift[l],
            *wl)
        h_s.append(hl); c_s.append(cl); S_s.append(Sl); sh_s.append(shl)
    mem_k_prompt = jnp.stack(mk_p)
    mem_v_prompt = jnp.stack(mv_p)
    lru_h_prompt = jnp.stack(h_p)
    lru_conv_prompt = jnp.stack(c_p)
    rwkv_S_prompt = jnp.stack(S_p)
    rwkv_shift_prompt = jnp.stack(sh_p)
    lru_h_sample = jnp.stack(h_s)
    lru_conv_sample = jnp.stack(c_s)
    rwkv_S_sample = jnp.stack(S_s)
    rwkv_shift_sample = jnp.stack(sh_s)
    return (yp, ys, mem_k_prompt, mem_v_prompt, lru_h_prompt, lru_conv_prompt, rwkv_S_prompt,
            rwkv_shift_prompt, lru_h_sample, lru_conv_sample, rwkv_S_sample, rwkv_shift_sample)
```

```python
import functools
import math

import jax
import jax.numpy as jnp
from jax import lax
from jax.experimental import pallas as pl
from jax.experimental.pallas import tpu as pltpu

F32 = jnp.float32
BF16 = jnp.bfloat16

D_MODEL = 2048
LRU_WIDTH = 1024
LRU_BLOCKS = 8
LRU_BLOCK = 128
CONV_WIDTH = 4
LRU_C = 8.0
RWKV_WIDTH = 1024
RWKV_HEAD = 64
RWKV_HEADS = 16
DECAY_LORA = 64
A_LORA = 64
G_LORA = 160
LORA_TOTAL = DECAY_LORA + A_LORA + G_LORA
SHIFT_WIDTH = 3 * RWKV_WIDTH + LORA_TOTAL
P_TOTAL = 2 * LRU_WIDTH + SHIFT_WIDTH
DECAY_SCALE = math.exp(-0.5)
N_MEM = 256
X_HEADS = 4
X_HEAD_DIM = 512
D_FF = 4 * D_MODEL
LN_EPS = 1e-5
GN_EPS = 64e-5
DEPTH = 1
ALPHA = (2 * DEPTH) ** 0.25

LANES = 128
SUBLANES = 8
VMEM_LIMIT_BYTES = 56 * 1024 * 1024

LORA_PAD = 512
P_PAD = 5 * 1024 + LORA_PAD
LORA_COL_BLOCK = (5 * 1024) // LORA_PAD
HEAD_PAIRS = RWKV_WIDTH // LANES


def _cparams(sem):
    return pltpu.CompilerParams(dimension_semantics=sem, vmem_limit_bytes=VMEM_LIMIT_BYTES)


def _mm_kernel(x_ref, w_ref, o_ref, xb_ref):
    @pl.when(pl.program_id(1) == 0)
    def _():
        xb_ref[...] = x_ref[...].astype(BF16)

    o_ref[...] = jnp.dot(xb_ref[...], w_ref[...], preferred_element_type=F32).astype(o_ref.dtype)


def _matmul(x, w, tm, tn, out_dtype=F32):
    M, K = x.shape
    N = w.shape[1]
    assert M % tm == 0 and N % tn == 0
    return pl.pallas_call(
        _mm_kernel,
        grid=(M // tm, N // tn),
        in_specs=[pl.BlockSpec((tm, K), lambda i, j: (i, 0)),
                  pl.BlockSpec((K, tn), lambda i, j: (0, j))],
        out_specs=pl.BlockSpec((tm, tn), lambda i, j: (i, j)),
        out_shape=jax.ShapeDtypeStruct((M, N), out_dtype),
        scratch_shapes=[pltpu.VMEM((tm, K), BF16)],
        compiler_params=_cparams(("arbitrary", "arbitrary")),
        name="matmul",
    )(x, w)


def _layer_norm(y, g, b):
    mu = jnp.mean(y, axis=-1, keepdims=True)
    d = y - mu
    var = jnp.mean(d * d, axis=-1, keepdims=True)
    return d * lax.rsqrt(var + LN_EPS) * g + b


def _mm_ln_kernel(n_in, *refs):
    x_refs = refs[:n_in]
    w_refs = refs[n_in:2 * n_in]
    res_ref, g_ref, b_ref, o_ref = refs[2 * n_in:]
    acc = jnp.dot(x_refs[0][...].astype(BF16), w_refs[0][...], preferred_element_type=F32)
    for x_ref, w_ref in zip(x_refs[1:], w_refs[1:]):
        acc = acc + jnp.dot(x_ref[...].astype(BF16), w_ref[...], preferred_element_type=F32)
    y = ALPHA * res_ref[...] + acc
    o_ref[...] = _layer_norm(y, g_ref[...], b_ref[...])


def _matmul_res_ln(xs, ws, res, g, b, tm):
    M, N = res.shape
    n_in = len(xs)
    in_specs = [pl.BlockSpec((tm, x.shape[1]), lambda i: (i, 0)) for x in xs]
    in_specs += [pl.BlockSpec(w.shape, lambda i: (0, 0)) for w in ws]
    in_specs += [pl.BlockSpec((tm, N), lambda i: (i, 0)),
                 pl.BlockSpec((1, N), lambda i: (0, 0)),
                 pl.BlockSpec((1, N), lambda i: (0, 0))]
    return pl.pallas_call(
        functools.partial(_mm_ln_kernel, n_in),
        grid=(M // tm,),
        in_specs=in_specs,
        out_specs=pl.BlockSpec((tm, N), lambda i: (i, 0)),
        out_shape=jax.ShapeDtypeStruct((M, N), F32),
        compiler_params=_cparams(("arbitrary",)),
        name="matmul_res_ln",
    )(*xs, *ws, res, g, b)


def _mlp_kernel(x_ref, w1_ref, w2_ref, g_ref, b_ref, o_ref, xb_ref, acc_ref):
    f = pl.program_id(1)

    @pl.when(f == 0)
    def _():
        xb_ref[...] = x_ref[...].astype(BF16)
        acc_ref[...] = jnp.zeros_like(acc_ref)

    h = jnp.dot(xb_ref[...], w1_ref[...], preferred_element_type=F32)
    h = jnp.square(jnp.maximum(h, 0.0))
    acc_ref[...] += jnp.dot(h.astype(BF16), w2_ref[...], preferred_element_type=F32)

    @pl.when(f == pl.num_programs(1) - 1)
    def _():
        y = ALPHA * x_ref[...] + acc_ref[...]
        o_ref[...] = _layer_norm(y, g_ref[...], b_ref[...])


def _mlp_ln(x, w1, w2, g, b, tm, tf):
    M, D = x.shape
    FF = w1.shape[1]
    return pl.pallas_call(
        _mlp_kernel,
        grid=(M // tm, FF // tf),
        in_specs=[pl.BlockSpec((tm, D), lambda i, f: (i, 0)),
                  pl.BlockSpec((D, tf), lambda i, f: (0, f)),
                  pl.BlockSpec((tf, D), lambda i, f: (f, 0)),
                  pl.BlockSpec((1, D), lambda i, f: (0, 0)),
                  pl.BlockSpec((1, D), lambda i, f: (0, 0))],
        out_specs=pl.BlockSpec((tm, D), lambda i, f: (i, 0)),
        out_shape=jax.ShapeDtypeStruct((M, D), F32),
        scratch_shapes=[pltpu.VMEM((tm, D), BF16), pltpu.VMEM((tm, D), F32)],
        compiler_params=_cparams(("arbitrary", "arbitrary")),
        name="mlp_ln",
    )(x, w1, w2, g, b)


def _attn_kernel(bb, q_ref, k_ref, v_ref, o_ref):
    scale = X_HEAD_DIM ** -0.5
    for b in range(bb):
        q = q_ref[b].astype(BF16)
        k = k_ref[b].astype(BF16)
        v = v_ref[b].astype(BF16)
        s = lax.dot_general(q, k, (((1,), (1,)), ((), ())), preferred_element_type=F32) * scale
        m = jnp.max(s, axis=-1, keepdims=True)
        e = jnp.exp(s - m)
        p = e / jnp.sum(e, axis=-1, keepdims=True)
        o_ref[b] = jnp.dot(p.astype(BF16), v, preferred_element_type=F32)


def _attention(q, mem_k, mem_v, bb, tq):
    B, T, D = q.shape
    return pl.pallas_call(
        functools.partial(_attn_kernel, bb),
        grid=(B // bb, X_HEADS, T // tq),
        in_specs=[pl.BlockSpec((bb, tq, X_HEAD_DIM), lambda b, h, t: (b, t, h)),
                  pl.BlockSpec((bb, N_MEM, X_HEAD_DIM), lambda b, h, t: (b, 0, h)),
                  pl.BlockSpec((bb, N_MEM, X_HEAD_DIM), lambda b, h, t: (b, 0, h))],
        out_specs=pl.BlockSpec((bb, tq, X_HEAD_DIM), lambda b, h, t: (b, t, h)),
        out_shape=jax.ShapeDtypeStruct((B, T, D), F32),
        compiler_params=_cparams(("arbitrary", "arbitrary", "arbitrary")),
        name="mem_attention",
    )(q, mem_k, mem_v)


def _softplus(x):
    return jnp.maximum(x, 0.0) + jnp.log1p(jnp.exp(-jnp.abs(x)))


def _lru_kernel(bb, tc, ux_ref, ug_ref, h0_ref, c0_ref, cw_ref, cb_ref, wa_ref, ba_ref, wx_ref, bx_ref,
                lam_ref, y_ref, hl_ref, cn_ref, ext_ref, hc_ref, a_ref, b_ref, hs_ref):
    ti = pl.program_id(1)
    pad = SUBLANES
    neg_c_sp = -LRU_C * _softplus(-lam_ref[...])

    for b in range(bb):
        @pl.when(ti == 0)
        def _():
            ext_ref[b, pad - 3:pad, :] = c0_ref[b]
            hc_ref[b] = h0_ref[b]

        ext_ref[b, pad:pad + tc, :] = ux_ref[b]
        xc = cb_ref[...] + cw_ref[0:1, :] * ext_ref[b, pad - 3:pad - 3 + tc, :]
        for j in range(1, CONV_WIDTH):
            xc = xc + cw_ref[j:j + 1, :] * ext_ref[b, pad - 3 + j:pad - 3 + j + tc, :]

        ga, gx = [], []
        for n in range(LRU_BLOCKS):
            xb = xc[:, n * LRU_BLOCK:(n + 1) * LRU_BLOCK].astype(BF16)
            ga.append(jnp.dot(xb, wa_ref[n], preferred_element_type=F32))
            gx.append(jnp.dot(xb, wx_ref[n], preferred_element_type=F32))
        r = jax.nn.sigmoid(jnp.concatenate(ga, axis=1) + ba_ref[...])
        i = jax.nn.sigmoid(jnp.concatenate(gx, axis=1) + bx_ref[...])
        log_a = neg_c_sp * r
        a_ref[...] = jnp.exp(log_a)
        b_ref[...] = jnp.sqrt(1.0 - jnp.exp(2.0 * log_a)) * (i * xc)

        def step(t, h):
            h = a_ref[pl.ds(t, 1), :] * h + b_ref[pl.ds(t, 1), :]
            hs_ref[pl.ds(t, 1), :] = h
            return h

        h = lax.fori_loop(0, tc, step, hc_ref[b])
        hc_ref[b] = h
        hl_ref[b] = h
        y_ref[b] = hs_ref[...] * jax.nn.gelu(ug_ref[b])
        tail = ext_ref[b, pad + tc - 3:pad + tc, :]
        cn_ref[b] = tail
        ext_ref[b, pad - 3:pad, :] = tail


def _lru(p3, h0, conv0, cw, cb, wa, ba, wx, bx, lam, bb, tc):
    B, T, _ = p3.shape
    W = LRU_WIDTH
    row = lambda: pl.BlockSpec((1, W), lambda b, t: (0, 0))
    blk = lambda: pl.BlockSpec((LRU_BLOCKS, LRU_BLOCK, LRU_BLOCK), lambda b, t: (0, 0, 0))
    return pl.pallas_call(
        functools.partial(_lru_kernel, bb, tc),
        grid=(B // bb, T // tc),
        in_specs=[pl.BlockSpec((bb, tc, W), lambda b, t: (b, t, 0)),
                  pl.BlockSpec((bb, tc, W), lambda b, t: (b, t, 1)),
                  pl.BlockSpec((bb, 1, W), lambda b, t: (b, 0, 0)),
                  pl.BlockSpec((bb, CONV_WIDTH - 1, W), lambda b, t: (b, 0, 0)),
                  pl.BlockSpec((CONV_WIDTH, W), lambda b, t: (0, 0)),
                  row(), blk(), row(), blk(), row(), row()],
        out_specs=[pl.BlockSpec((bb, tc, W), lambda b, t: (b, t, 0)),
                   pl.BlockSpec((bb, 1, W), lambda b, t: (b, 0, 0)),
                   pl.BlockSpec((bb, CONV_WIDTH - 1, W), lambda b, t: (b, 0, 0))],
        out_shape=[jax.ShapeDtypeStruct((B, T, W), F32),
                   jax.ShapeDtypeStruct((B, 1, W), F32),
                   jax.ShapeDtypeStruct((B, CONV_WIDTH - 1, W), F32)],
        scratch_shapes=[pltpu.VMEM((bb, tc + SUBLANES, W), F32),
                        pltpu.VMEM((bb, 1, W), F32),
                        pltpu.VMEM((tc, W), F32),
                        pltpu.VMEM((tc, W), F32),
                        pltpu.VMEM((tc, W), F32)],
        compiler_params=_cparams(("arbitrary", "arbitrary")),
        name="rg_lru",
    )(p3, p3, h0, conv0, cw, cb, wa, ba, wx, bx, lam)


def _pair_ones():
    ri = lax.broadcasted_iota(jnp.int32, (LANES, LANES), 0)
    ci = lax.broadcasted_iota(jnp.int32, (LANES, LANES), 1)
    return ((ri >> 6) == (ci >> 6)).astype(BF16)


def _pair_eye():
    ri = lax.broadcasted_iota(jnp.int32, (RWKV_HEAD, LANES), 0)
    ci = lax.broadcasted_iota(jnp.int32, (RWKV_HEAD, LANES), 1)
    return ((ci & (RWKV_HEAD - 1)) == ri).astype(F32)


def _head_sum(x, ones):
    outs = []
    for c in range(HEAD_PAIRS):
        xc = x[:, c * LANES:(c + 1) * LANES]
        hi = xc.astype(BF16)
        r1 = xc - hi.astype(F32)
        mid = r1.astype(BF16)
        lo = (r1 - mid.astype(F32)).astype(BF16)
        s = jnp.dot(hi, ones, preferred_element_type=F32)
        s = s + jnp.dot(mid, ones, preferred_element_type=F32)
        s = s + jnp.dot(lo, ones, preferred_element_type=F32)
        outs.append(s)
    return jnp.concatenate(outs, axis=1)


def _rwkv_kernel(bb, tc, ur_ref, uk_ref, uv_ref, ul_ref, sh0_ref, s0_ref, mu_ref, w0_ref, ww_ref, a0_ref,
                 wa_ref, wg_ref, kk_ref, ka_ref, rk_ref, lnw_ref, lnb_ref,
                 y_ref, st_ref,
                 s_ref, carry_ref, kap_ref, kapa_ref, w_ref, km_ref, v_ref, r_ref, o_ref):
    ti = pl.program_id(1)
    ones = _pair_ones()
    eye = _pair_eye()
    W = RWKV_WIDTH
    row0 = lax.broadcasted_iota(jnp.int32, (tc, 1), 0) == 0

    @pl.when(ti == 0)
    def _():
        s_ref[...] = s0_ref[...]
        carry_ref[...] = sh0_ref[...]

    def shifted(u, b, lo, width):
        prev = jnp.where(row0, carry_ref[b, :, lo:lo + width], pltpu.roll(u, shift=1, axis=0))
        return u + (prev - u) * mu_ref[:, lo:lo + width]

    gates = []
    for b in range(bb):
        ur, uk, uv, ul = ur_ref[b], uk_ref[b], uv_ref[b], ul_ref[b]
        zr = shifted(ur, b, 0, W)
        zk = shifted(uk, b, W, W)
        zv = shifted(uv, b, 2 * W, W)
        zl = shifted(ul, b, 3 * W, LORA_PAD)
        carry_ref[b, :, 0:W] = ur[tc - 1:tc, :]
        carry_ref[b, :, W:2 * W] = uk[tc - 1:tc, :]
        carry_ref[b, :, 2 * W:3 * W] = uv[tc - 1:tc, :]
        carry_ref[b, :, 3 * W:3 * W + LORA_PAD] = ul[tc - 1:tc, :]

        dw = jnp.dot(jnp.tanh(zl).astype(BF16), ww_ref[...], preferred_element_type=F32)
        da = jnp.dot(zl.astype(BF16), wa_ref[...], preferred_element_type=F32)
        g = jnp.dot(jax.nn.sigmoid(zl).astype(BF16), wg_ref[...], preferred_element_type=F32)
        w = jnp.exp(-DECAY_SCALE * jax.nn.sigmoid(w0_ref[...] + dw))
        a = jax.nn.sigmoid(a0_ref[...] + da)
        kk = zk * kk_ref[...]
        kap = kk * lax.rsqrt(jnp.maximum(_head_sum(kk * kk, ones), 1e-24))
        km = zk * (1.0 + (a - 1.0) * ka_ref[...])
        rs = slice(b * tc, (b + 1) * tc)
        kap_ref[rs, :] = kap
        kapa_ref[rs, :] = kap * a
        w_ref[rs, :] = w
        km_ref[rs, :] = km
        v_ref[rs, :] = zv
        r_ref[rs, :] = zr
        gates.append(g)

    def step(t, carry):
        for b in range(bb):
            row = pl.ds(b * tc + t, 1)
            kap_t, kapa_t, w_t = kap_ref[row, :], kapa_ref[row, :], w_ref[row, :]
            km_t, v_t, r_t = km_ref[row, :], v_ref[row, :], r_ref[row, :]
            lhs = []
            for c in range(HEAD_PAIRS):
                sl = slice(c * LANES, (c + 1) * LANES)
                lhs.append((s_ref[b, c] * kap_t[:, sl]).astype(BF16))
                lhs.append((eye * v_t[:, sl]).astype(BF16))
            red = jnp.dot(jnp.concatenate(lhs, axis=0), ones, preferred_element_type=F32)
            qs = []
            for c in range(HEAD_PAIRS):
                sl = slice(c * LANES, (c + 1) * LANES)
                skk = red[(2 * c) * RWKV_HEAD:(2 * c + 1) * RWKV_HEAD]
                vb = red[(2 * c + 1) * RWKV_HEAD:(2 * c + 2) * RWKV_HEAD]
                s_new = s_ref[b, c] * w_t[:, sl] - skk * kapa_t[:, sl] + vb * km_t[:, sl]
                s_ref[b, c] = s_new
                qs.append((s_new * r_t[:, sl]).astype(BF16))
            ob = jnp.dot(jnp.concatenate(qs, axis=0), ones, preferred_element_type=F32)
            o_rows = [jnp.sum(ob[c * RWKV_HEAD:(c + 1) * RWKV_HEAD] * eye, axis=0, keepdims=True)
                      for c in range(HEAD_PAIRS)]
            o_ref[row, :] = jnp.concatenate(o_rows, axis=1)
        return carry

    lax.fori_loop(0, tc, step, 0)
    st_ref[...] = s_ref[...]

    inv_n = 1.0 / RWKV_HEAD
    for b in range(bb):
        rs = slice(b * tc, (b + 1) * tc)
        o = o_ref[rs, :]
        mean = _head_sum(o, ones) * inv_n
        d = o - mean
        var = _head_sum(d * d, ones) * inv_n
        on = d * lax.rsqrt(var + GN_EPS) * lnw_ref[...] + lnb_ref[...]
        bonus = _head_sum(r_ref[rs, :] * km_ref[rs, :] * rk_ref[...], ones) * v_ref[rs, :]
        y_ref[b] = (on + bonus) * gates[b]


def _rwkv(p3, shift0, s0, mu, w0, ww, a0, wa, wg, kk, ka, rk, lnw, lnb, bb, tc):
    B, T, _ = p3.shape
    W = RWKV_WIDTH
    SW = 3 * W + LORA_PAD
    row = lambda: pl.BlockSpec((1, W), lambda b, t: (0, 0))
    lora = lambda: pl.BlockSpec((LORA_PAD, W), lambda b, t: (0, 0))
    chunk = lambda: pltpu.VMEM((bb * tc, W), F32)
    return pl.pallas_call(
        functools.partial(_rwkv_kernel, bb, tc),
        grid=(B // bb, T // tc),
        in_specs=[pl.BlockSpec((bb, tc, W), lambda b, t: (b, t, 2)),
                  pl.BlockSpec((bb, tc, W), lambda b, t: (b, t, 3)),
                  pl.BlockSpec((bb, tc, W), lambda b, t: (b, t, 4)),
                  pl.BlockSpec((bb, tc, LORA_PAD), lambda b, t: (b, t, LORA_COL_BLOCK)),
                  pl.BlockSpec((bb, 1, SW), lambda b, t: (b, 0, 0)),
                  pl.BlockSpec((bb, HEAD_PAIRS, RWKV_HEAD, LANES), lambda b, t: (b, 0, 0, 0)),
                  pl.BlockSpec((1, SW), lambda b, t: (0, 0)),
                  row(), lora(), row(), lora(), lora(), row(), row(), row(), row(), row()],
        out_specs=[pl.BlockSpec((bb, tc, W), lambda b, t: (b, t, 0)),
                   pl.BlockSpec((bb, HEAD_PAIRS, RWKV_HEAD, LANES), lambda b, t: (b, 0, 0, 0))],
        out_shape=[jax.ShapeDtypeStruct((B, T, W), F32),
                   jax.ShapeDtypeStruct((B, HEAD_PAIRS, RWKV_HEAD, LANES), F32)],
        scratch_shapes=[pltpu.VMEM((bb, HEAD_PAIRS, RWKV_HEAD, LANES), F32),
                        pltpu.VMEM((bb, 1, SW), F32),
                        chunk(), chunk(), chunk(), chunk(), chunk(), chunk(), chunk()],
        compiler_params=_cparams(("arbitrary", "arbitrary")),
        name="rwkv7",
    )(p3, p3, p3, p3, shift0, s0, mu, w0, ww, a0, wa, wg, kk, ka, rk, lnw, lnb)


def _pack_state(s):
    B = s.shape[0]
    s = s.reshape(B, HEAD_PAIRS, 2, RWKV_HEAD, RWKV_HEAD)
    return jnp.transpose(s, (0, 1, 3, 2, 4)).reshape(B, HEAD_PAIRS, RWKV_HEAD, LANES)


def _unpack_state(s):
    B = s.shape[0]
    s = s.reshape(B, HEAD_PAIRS, RWKV_HEAD, 2, RWKV_HEAD)
    return jnp.transpose(s, (0, 1, 3, 2, 4)).reshape(B, RWKV_HEADS, RWKV_HEAD, RWKV_HEAD)


def _pad_shift(a):
    pad = [(0, 0)] * (a.ndim - 1) + [(0, LORA_PAD - LORA_TOTAL)]
    return jnp.pad(a, pad)


def _lora_rows(w_up, offset):
    rank = w_up.shape[0]
    return jnp.pad(w_up, ((offset, LORA_PAD - offset - rank), (0, 0))).astype(BF16)


def _hybrid_layer(x, mem_k, mem_v, h0, conv0, S0, shift0, wts, cfg):
    B, T, D = x.shape
    M = B * T
    x2 = x.reshape(M, D)

    p = _matmul(x2, wts["w_in"], cfg["tm"], 512)
    p3 = p.reshape(B, T, P_PAD)

    y_lru, h_last, conv_new = _lru(p3, h0.reshape(B, 1, LRU_WIDTH), conv0,
                                   wts["conv_w"], wts["conv_b"], wts["lru_wa"], wts["lru_ba"],
                                   wts["lru_wx"], wts["lru_bx"], wts["lru_L"], cfg["lru_bb"], cfg["lru_tc"])

    y_rwkv, s_new = _rwkv(p3, _pad_shift(shift0).reshape(B, 1, -1), _pack_state(S0),
                          wts["mu"], wts["w0"], wts["ww"], wts["a0"], wts["wa"], wts["wg"],
                          wts["k_k"], wts["k_a"], wts["r_k"], wts["ln_w"], wts["ln_b"],
                          cfg["rwkv_bb"], cfg["rwkv_tc"])
    shift_new = p3[:, T - 1, 2 * LRU_WIDTH:P_TOTAL]

    x1 = _matmul_res_ln([y_lru.reshape(M, LRU_WIDTH), y_rwkv.reshape(M, RWKV_WIDTH)],
                        [wts["w_out_a"], wts["w_out_b"]], x2, wts["ln1_g"], wts["ln1_b"], cfg["tm_ln"])

    q = _matmul(x1, wts["wq"], cfg["tm"], 512)
    att = _attention(q.reshape(B, T, D), mem_k, mem_v, cfg["att_bb"], cfg["att_tq"])
    x2n = _matmul_res_ln([att.reshape(M, D)], [wts["wo"]], x1, wts["ln2_g"], wts["ln2_b"], cfg["tm_ln"])

    x3 = _mlp_ln(x2n, wts["w1"], wts["w2"], wts["ln3_g"], wts["ln3_b"], cfg["tm"], 512)
    return (x3.reshape(B, T, D), h_last.reshape(B, LRU_WIDTH), conv_new,
            _unpack_state(s_new), shift_new)


def kernel(x_prompt, x_sample, mem_prompt, cache_mem_k, cache_mem_v, state_lru_h, state_lru_conv, state_rwkv_S, state_rwkv_shift, w_in, lru_conv_w, lru_conv_b, lru_wa, lru_ba, lru_wx, lru_bx, lru_L, rwkv_mu, rwkv_w0, rwkv_w_up, rwkv_a0, rwkv_a_up, rwkv_g_up, rwkv_k_k, rwkv_k_a, rwkv_r_k, rwkv_ln_w, rwkv_ln_b, w_out, ln1_g, ln1_b, xa_wq, xa_wk, xa_wv, xa_wo, ln2_g, ln2_b, mlp_w1, mlp_w2, ln3_g, ln3_b):
    B, T, D = x_prompt.shape
    Bs, Ts, _ = x_sample.shape
    assert w_in.shape[0] == DEPTH

    cfg_p = dict(tm=512, tm_ln=256, lru_bb=1, lru_tc=256, rwkv_bb=B, rwkv_tc=64, att_bb=1, att_tq=512)
    cfg_s = dict(tm=512, tm_ln=256, lru_bb=8, lru_tc=Ts, rwkv_bb=4, rwkv_tc=Ts, att_bb=4, att_tq=Ts)

    yp, ys = x_prompt, x_sample
    outs_p = [[] for _ in range(6)]
    outs_s = [[] for _ in range(4)]
    for l in range(DEPTH):
        row = lambda a: a[l].reshape(1, -1)
        wts = dict(
            w_in=jnp.pad(w_in[l], ((0, 0), (0, P_PAD - P_TOTAL))).astype(BF16),
            conv_w=lru_conv_w[l], conv_b=row(lru_conv_b),
            lru_wa=lru_wa[l].astype(BF16), lru_ba=row(lru_ba),
            lru_wx=lru_wx[l].astype(BF16), lru_bx=row(lru_bx), lru_L=row(lru_L),
            mu=_pad_shift(rwkv_mu[l]).reshape(1, -1), w0=row(rwkv_w0), a0=row(rwkv_a0),
            ww=_lora_rows(rwkv_w_up[l], 0), wa=_lora_rows(rwkv_a_up[l], DECAY_LORA),
            wg=_lora_rows(rwkv_g_up[l], DECAY_LORA + A_LORA),
            k_k=row(rwkv_k_k), k_a=row(rwkv_k_a), r_k=row(rwkv_r_k), ln_w=row(rwkv_ln_w), ln_b=row(rwkv_ln_b),
            w_out_a=w_out[l, :LRU_WIDTH].astype(BF16), w_out_b=w_out[l, LRU_WIDTH:].astype(BF16),
            ln1_g=row(ln1_g), ln1_b=row(ln1_b),
            wq=xa_wq[l].astype(BF16), wo=xa_wo[l].astype(BF16), ln2_g=row(ln2_g), ln2_b=row(ln2_b),
            w1=mlp_w1[l].astype(BF16), w2=mlp_w2[l].astype(BF16), ln3_g=row(ln3_g), ln3_b=row(ln3_b),
        )
        mem2 = mem_prompt.reshape(B * N_MEM, D)
        mem_k = _matmul(mem2, xa_wk[l].astype(BF16), 512, 512).reshape(B, N_MEM, D)
        mem_v = _matmul(mem2, xa_wv[l].astype(BF16), 512, 512).reshape(B, N_MEM, D)
        yp, hl, cl, Sl, shl = _hybrid_layer(
            yp, mem_k, mem_v,
            jnp.zeros((B, LRU_WIDTH), F32), jnp.zeros((B, CONV_WIDTH - 1, LRU_WIDTH), F32),
            jnp.zeros((B, RWKV_HEADS, RWKV_HEAD, RWKV_HEAD), F32), jnp.zeros((B, SHIFT_WIDTH), F32),
            wts, cfg_p)
        for lst, val in zip(outs_p, (mem_k.reshape(B, N_MEM, X_HEADS, X_HEAD_DIM),
                                     mem_v.reshape(B, N_MEM, X_HEADS, X_HEAD_DIM), hl, cl, Sl, shl)):
            lst.append(val)
        ys, hl, cl, Sl, shl = _hybrid_layer(
            ys, cache_mem_k[l].reshape(Bs, N_MEM, D), cache_mem_v[l].reshape(Bs, N_MEM, D),
            state_lru_h[l], state_lru_conv[l], state_rwkv_S[l], state_rwkv_shift[l], wts, cfg_s)
        for lst, val in zip(outs_s, (hl, cl, Sl, shl)):
            lst.append(val)

    return (yp, ys, *(jnp.stack(o) for o in outs_p), *(jnp.stack(o) for o in outs_s))
```

```python
import functools
import math

import jax
import jax.numpy as jnp
from jax import lax
from jax.experimental import pallas as pl
from jax.experimental.pallas import tpu as pltpu

F32 = jnp.float32
BF16 = jnp.bfloat16

D_MODEL = 2048
LRU_WIDTH = 1024
LRU_BLOCKS = 8
LRU_BLOCK = 128
CONV_WIDTH = 4
LRU_C = 8.0
RWKV_WIDTH = 1024
RWKV_HEAD = 64
RWKV_HEADS = 16
DECAY_LORA = 64
A_LORA = 64
G_LORA = 160
LORA_TOTAL = DECAY_LORA + A_LORA + G_LORA
SHIFT_WIDTH = 3 * RWKV_WIDTH + LORA_TOTAL
P_TOTAL = 2 * LRU_WIDTH + SHIFT_WIDTH
DECAY_SCALE = math.exp(-0.5)
N_MEM = 256
X_HEADS = 4
X_HEAD_DIM = 512
D_FF = 4 * D_MODEL
LN_EPS = 1e-5
GN_EPS = 64e-5
DEPTH = 1
ALPHA = (2 * DEPTH) ** 0.25

LANES = 128
SUBLANES = 8
VMEM_LIMIT_BYTES = 56 * 1024 * 1024

LORA_PAD = 512
P_PAD = 5 * 1024 + LORA_PAD
LORA_COL_BLOCK = (5 * 1024) // LORA_PAD
HEAD_PAIRS = RWKV_WIDTH // LANES


def _cparams(sem):
    return pltpu.CompilerParams(dimension_semantics=sem, vmem_limit_bytes=VMEM_LIMIT_BYTES)


def _mm_kernel(x_ref, w_ref, o_ref, xb_ref):
    @pl.when(pl.program_id(1) == 0)
    def _():
        xb_ref[...] = x_ref[...].astype(BF16)

    o_ref[...] = jnp.dot(xb_ref[...], w_ref[...], preferred_element_type=F32).astype(o_ref.dtype)


def _matmul(x, w, tm, tn, out_dtype=F32):
    M, K = x.shape
    N = w.shape[1]
    assert M % tm == 0 and N % tn == 0
    return pl.pallas_call(
        _mm_kernel,
        grid=(M // tm, N // tn),
        in_specs=[pl.BlockSpec((tm, K), lambda i, j: (i, 0)),
                  pl.BlockSpec((K, tn), lambda i, j: (0, j))],
        out_specs=pl.BlockSpec((tm, tn), lambda i, j: (i, j)),
        out_shape=jax.ShapeDtypeStruct((M, N), out_dtype),
        scratch_shapes=[pltpu.VMEM((tm, K), BF16)],
        compiler_params=_cparams(("arbitrary", "arbitrary")),
        name="matmul",
    )(x, w)


def _layer_norm(y, g, b):
    mu = jnp.mean(y, axis=-1, keepdims=True)
    d = y - mu
    var = jnp.mean(d * d, axis=-1, keepdims=True)
    return d * lax.rsqrt(var + LN_EPS) * g + b


def _mm_ln_kernel(n_in, *refs):
    x_refs = refs[:n_in]
    w_refs = refs[n_in:2 * n_in]
    res_ref, g_ref, b_ref, o_ref = refs[2 * n_in:]
    acc = jnp.dot(x_refs[0][...].astype(BF16), w_refs[0][...], preferred_element_type=F32)
    for x_ref, w_ref in zip(x_refs[1:], w_refs[1:]):
        acc = acc + jnp.dot(x_ref[...].astype(BF16), w_ref[...], preferred_element_type=F32)
    y = ALPHA * res_ref[...] + acc
    o_ref[...] = _layer_norm(y, g_ref[...], b_ref[...])


def _matmul_res_ln(xs, ws, res, g, b, tm):
    M, N = res.shape
    n_in = len(xs)
    in_specs = [pl.BlockSpec((tm, x.shape[1]), lambda i: (i, 0)) for x in xs]
    in_specs += [pl.BlockSpec(w.shape, lambda i: (0, 0)) for w in ws]
    in_specs += [pl.BlockSpec((tm, N), lambda i: (i, 0)),
                 pl.BlockSpec((1, N), lambda i: (0, 0)),
                 pl.BlockSpec((1, N), lambda i: (0, 0))]
    return pl.pallas_call(
        functools.partial(_mm_ln_kernel, n_in),
        grid=(M // tm,),
        in_specs=in_specs,
        out_specs=pl.BlockSpec((tm, N), lambda i: (i, 0)),
        out_shape=jax.ShapeDtypeStruct((M, N), F32),
        compiler_params=_cparams(("arbitrary",)),
        name="matmul_res_ln",
    )(*xs, *ws, res, g, b)


def _mlp_kernel(x_ref, w1_ref, w2_ref, g_ref, b_ref, o_ref, xb_ref, acc_ref):
    f = pl.program_id(1)

    @pl.when(f == 0)
    def _():
        xb_ref[...] = x_ref[...].astype(BF16)
        acc_ref[...] = jnp.zeros_like(acc_ref)

    h = jnp.dot(xb_ref[...], w1_ref[...], preferred_element_type=F32)
    h = jnp.square(jnp.maximum(h, 0.0))
    acc_ref[...] += jnp.dot(h.astype(BF16), w2_ref[...], preferred_element_type=F32)

    @pl.when(f == pl.num_programs(1) - 1)
    def _():
        y = ALPHA * x_ref[...] + acc_ref[...]
        o_ref[...] = _layer_norm(y, g_ref[...], b_ref[...])


def _mlp_ln(x, w1, w2, g, b, tm, tf):
    M, D = x.shape
    FF = w1.shape[1]
    return pl.pallas_call(
        _mlp_kernel,
        grid=(M // tm, FF // tf),
        in_specs=[pl.BlockSpec((tm, D), lambda i, f: (i, 0)),
                  pl.BlockSpec((D, tf), lambda i, f: (0, f)),
                  pl.BlockSpec((tf, D), lambda i, f: (f, 0)),
                  pl.BlockSpec((1, D), lambda i, f: (0, 0)),
                  pl.BlockSpec((1, D), lambda i, f: (0, 0))],
        out_specs=pl.BlockSpec((tm, D), lambda i, f: (i, 0)),
        out_shape=jax.ShapeDtypeStruct((M, D), F32),
        scratch_shapes=[pltpu.VMEM((tm, D), BF16), pltpu.VMEM((tm, D), F32)],
        compiler_params=_cparams(("arbitrary", "arbitrary")),
        name="mlp_ln",
    )(x, w1, w2, g, b)


def _attn_kernel(bb, q_ref, k_ref, v_ref, o_ref):
    scale = X_HEAD_DIM ** -0.5
    for b in range(bb):
        q = q_ref[b].astype(BF16)
        k = k_ref[b].astype(BF16)
        v = v_ref[b].astype(BF16)
        s = lax.dot_general(q, k, (((1,), (1,)), ((), ())), preferred_element_type=F32) * scale
        m = jnp.max(s, axis=-1, keepdims=True)
        e = jnp.exp(s - m)
        p = e / jnp.sum(e, axis=-1, keepdims=True)
        o_ref[b] = jnp.dot(p.astype(BF16), v, preferred_element_type=F32)


def _attention(q, mem_k, mem_v, bb, tq):
    B, T, D = q.shape
    return pl.pallas_call(
        functools.partial(_attn_kernel, bb),
        grid=(B // bb, X_HEADS, T // tq),
        in_specs=[pl.BlockSpec((bb, tq, X_HEAD_DIM), lambda b, h, t: (b, t, h)),
                  pl.BlockSpec((bb, N_MEM, X_HEAD_DIM), lambda b, h, t: (b, 0, h)),
                  pl.BlockSpec((bb, N_MEM, X_HEAD_DIM), lambda b, h, t: (b, 0, h))],
        out_specs=pl.BlockSpec((bb, tq, X_HEAD_DIM), lambda b, h, t: (b, t, h)),
        out_shape=jax.ShapeDtypeStruct((B, T, D), F32),
        compiler_params=_cparams(("arbitrary", "arbitrary", "arbitrary")),
        name="mem_attention",
    )(q, mem_k, mem_v)


def _softplus(x):
    return jnp.maximum(x, 0.0) + jnp.log1p(jnp.exp(-jnp.abs(x)))


def _lru_kernel(bb, tc, ux_ref, ug_ref, h0_ref, c0_ref, cw_ref, cb_ref, wa_ref, ba_ref, wx_ref, bx_ref,
                lam_ref, y_ref, hl_ref, cn_ref, ext_ref, hc_ref, a_ref, b_ref, hs_ref):
    ti = pl.program_id(1)
    pad = SUBLANES
    neg_c_sp = -LRU_C * _softplus(-lam_ref[...])

    for b in range(bb):
        @pl.when(ti == 0)
        def _():
            ext_ref[b, pad - 3:pad, :] = c0_ref[b]
            hc_ref[b] = h0_ref[b]

        ext_ref[b, pad:pad + tc, :] = ux_ref[b]
        xc = cb_ref[...] + cw_ref[0:1, :] * ext_ref[b, pad - 3:pad - 3 + tc, :]
        for j in range(1, CONV_WIDTH):
            xc = xc + cw_ref[j:j + 1, :] * ext_ref[b, pad - 3 + j:pad - 3 + j + tc, :]

        ga, gx = [], []
        for n in range(LRU_BLOCKS):
            xb = xc[:, n * LRU_BLOCK:(n + 1) * LRU_BLOCK].astype(BF16)
            ga.append(jnp.dot(xb, wa_ref[n], preferred_element_type=F32))
            gx.append(jnp.dot(xb, wx_ref[n], preferred_element_type=F32))
        r = jax.nn.sigmoid(jnp.concatenate(ga, axis=1) + ba_ref[...])
        i = jax.nn.sigmoid(jnp.concatenate(gx, axis=1) + bx_ref[...])
        log_a = neg_c_sp * r
        a_ref[...] = jnp.exp(log_a)
        b_ref[...] = jnp.sqrt(1.0 - jnp.exp(2.0 * log_a)) * (i * xc)

        def step(t, h):
            h = a_ref[pl.ds(t, 1), :] * h + b_ref[pl.ds(t, 1), :]
            hs_ref[pl.ds(t, 1), :] = h
            return h

        h = lax.fori_loop(0, tc, step, hc_ref[b])
        hc_ref[b] = h
        hl_ref[b] = h
        y_ref[b] = hs_ref[...] * jax.nn.gelu(ug_ref[b])
        tail = ext_ref[b, pad + tc - 3:pad + tc, :]
        cn_ref[b] = tail
        ext_ref[b, pad - 3:pad, :] = tail


def _lru(p3, h0, conv0, cw, cb, wa, ba, wx, bx, lam, bb, tc):
    B, T, _ = p3.shape
    W = LRU_WIDTH
    row = lambda: pl.BlockSpec((1, W), lambda b, t: (0, 0))
    blk = lambda: pl.BlockSpec((LRU_BLOCKS, LRU_BLOCK, LRU_BLOCK), lambda b, t: (0, 0, 0))
    return pl.pallas_call(
        functools.partial(_lru_kernel, bb, tc),
        grid=(B // bb, T // tc),
        in_specs=[pl.BlockSpec((bb, tc, W), lambda b, t: (b, t, 0)),
                  pl.BlockSpec((bb, tc, W), lambda b, t: (b, t, 1)),
                  pl.BlockSpec((bb, 1, W), lambda b, t: (b, 0, 0)),
                  pl.BlockSpec((bb, CONV_WIDTH - 1, W), lambda b, t: (b, 0, 0)),
                  pl.BlockSpec((CONV_WIDTH, W), lambda b, t: (0, 0)),
                  row(), blk(), row(), blk(), row(), row()],
        out_specs=[pl.BlockSpec((bb, tc, W), lambda b, t: (b, t, 0)),
                   pl.BlockSpec((bb, 1, W), lambda b, t: (b, 0, 0)),
                   pl.BlockSpec((bb, CONV_WIDTH - 1, W), lambda b, t: (b, 0, 0))],
        out_shape=[jax.ShapeDtypeStruct((B, T, W), F32),
                   jax.ShapeDtypeStruct((B, 1, W), F32),
                   jax.ShapeDtypeStruct((B, CONV_WIDTH - 1, W), F32)],
        scratch_shapes=[pltpu.VMEM((bb, tc + SUBLANES, W), F32),
                        pltpu.VMEM((bb, 1, W), F32),
                        pltpu.VMEM((tc, W), F32),
                        pltpu.VMEM((tc, W), F32),
                        pltpu.VMEM((tc, W), F32)],
        compiler_params=_cparams(("arbitrary", "arbitrary")),
        name="rg_lru",
    )(p3, p3, h0, conv0, cw, cb, wa, ba, wx, bx, lam)


def _head_ones(n):
    ri = lax.broadcasted_iota(jnp.int32, (n, n), 0)
    ci = lax.broadcasted_iota(jnp.int32, (n, n), 1)
    return ((ri >> 6) == (ci >> 6)).astype(BF16)


def _head_sum(x, ones):
    outs = []
    for c in range(HEAD_PAIRS):
        xc = x[:, c * LANES:(c + 1) * LANES]
        hi = xc.astype(BF16)
        r1 = xc - hi.astype(F32)
        mid = r1.astype(BF16)
        lo = (r1 - mid.astype(F32)).astype(BF16)
        s = jnp.dot(hi, ones, preferred_element_type=F32)
        s = s + jnp.dot(mid, ones, preferred_element_type=F32)
        s = s + jnp.dot(lo, ones, preferred_element_type=F32)
        outs.append(s)
    return jnp.concatenate(outs, axis=1)


def _rwkv_kernel(bb, tc, ur_ref, uk_ref, uv_ref, ul_ref, sh0_ref, s0_ref, mu_ref, w0_ref, ww_ref, a0_ref,
                 wa_ref, wg_ref, kk_ref, ka_ref, rk_ref, lnw_ref, lnb_ref,
                 y_ref, st_ref,
                 s_ref, carry_ref, kap_ref, kapa_ref, w_ref, km_ref, v_ref, r_ref, ob_ref):
    ti = pl.program_id(1)
    ones = _head_ones(LANES)
    ones2 = _head_ones(2 * LANES)
    W = RWKV_WIDTH
    NH = RWKV_HEAD
    row0 = lax.broadcasted_iota(jnp.int32, (tc, 1), 0) == 0
    sub_i = lax.broadcasted_iota(jnp.int32, (NH, LANES), 0)
    lane_i = lax.broadcasted_iota(jnp.int32, (NH, LANES), 1)
    lane_t = lane_i & (NH - 1)
    eye16 = (lane_t == sub_i).astype(BF16)
    gb = 2 if bb % 2 == 0 else 1

    @pl.when(ti == 0)
    def _():
        s_ref[...] = s0_ref[...]
        carry_ref[...] = sh0_ref[...]
        ob_ref[...] = jnp.zeros_like(ob_ref)

    def shifted(u, b, lo, width):
        prev = jnp.where(row0, carry_ref[b, :, lo:lo + width], pltpu.roll(u, shift=1, axis=0))
        return u + (prev - u) * mu_ref[:, lo:lo + width]

    gates = []
    for b in range(bb):
        ur, uk, uv, ul = ur_ref[b], uk_ref[b], uv_ref[b], ul_ref[b]
        zr = shifted(ur, b, 0, W)
        zk = shifted(uk, b, W, W)
        zv = shifted(uv, b, 2 * W, W)
        zl = shifted(ul, b, 3 * W, LORA_PAD)
        carry_ref[b, :, 0:W] = ur[tc - 1:tc, :]
        carry_ref[b, :, W:2 * W] = uk[tc - 1:tc, :]
        carry_ref[b, :, 2 * W:3 * W] = uv[tc - 1:tc, :]
        carry_ref[b, :, 3 * W:3 * W + LORA_PAD] = ul[tc - 1:tc, :]

        dw = jnp.dot(jnp.tanh(zl).astype(BF16), ww_ref[...], preferred_element_type=F32)
        da = jnp.dot(zl.astype(BF16), wa_ref[...], preferred_element_type=F32)
        g = jnp.dot(jax.nn.sigmoid(zl).astype(BF16), wg_ref[...], preferred_element_type=F32)
        w = jnp.exp(-DECAY_SCALE * jax.nn.sigmoid(w0_ref[...] + dw))
        a = jax.nn.sigmoid(a0_ref[...] + da)
        kk = zk * kk_ref[...]
        kap = kk * lax.rsqrt(jnp.maximum(_head_sum(kk * kk, ones), 1e-24))
        km = zk * (1.0 + (a - 1.0) * ka_ref[...])
        rs = slice(b * tc, (b + 1) * tc)
        kap_ref[rs, :] = kap
        kapa_ref[rs, :] = kap * a
        w_ref[rs, :] = w
        km_ref[rs, :] = km
        v_ref[rs, :] = zv
        r_ref[rs, :] = zr
        gates.append(g)

    def lanes(c):
        return slice(c * LANES, (c + 1) * LANES)

    def out_products(t_out, with_update, t):
        omask = lane_t == t_out
        for g0 in range(0, bb, gb):
            group = range(g0, g0 + gb)
            lhs, rows = [], []
            for b in group:
                r_p = r_ref[pl.ds(b * tc + jnp.maximum(t_out, 0), 1), :]
                if with_update:
                    row = pl.ds(b * tc + t, 1)
                    kap_t = kap_ref[row, :]
                    rows.append((kapa_ref[row, :], w_ref[row, :], km_ref[row, :]))
                    for c in range(HEAD_PAIRS):
                        s = s_ref[b, c]
                        lhs.append(jnp.concatenate([(s * kap_t[:, lanes(c)]).astype(BF16),
                                                    (s * r_p[:, lanes(c)]).astype(BF16)], axis=1))
                else:
                    for c in range(0, HEAD_PAIRS, 2):
                        lhs.append(jnp.concatenate([(s_ref[b, c] * r_p[:, lanes(c)]).astype(BF16),
                                                    (s_ref[b, c + 1] * r_p[:, lanes(c + 1)]).astype(BF16)], axis=1))
            if with_update:
                for b in group:
                    v_t = v_ref[pl.ds(b * tc + t, 1), :]
                    dv = [eye16 * jnp.broadcast_to(v_t[:, lanes(c)], (NH, LANES)).astype(BF16)
                          for c in range(HEAD_PAIRS)]
                    for c in range(0, HEAD_PAIRS, 2):
                        lhs.append(jnp.concatenate([dv[c], dv[c + 1]], axis=1))
            red = jnp.dot(jnp.concatenate(lhs, axis=0), ones2, preferred_element_type=F32)
            for bi, b in enumerate(group):
                for c in range(HEAD_PAIRS):
                    if with_update:
                        base = (bi * HEAD_PAIRS + c) * NH
                        skk = red[base:base + NH, 0:LANES]
                        ob = red[base:base + NH, LANES:2 * LANES]
                        vbase = (gb * HEAD_PAIRS + bi * (HEAD_PAIRS // 2) + c // 2) * NH
                        vb = red[vbase:vbase + NH, (c % 2) * LANES:(c % 2 + 1) * LANES]
                        kapa_t, w_t, km_t = rows[bi]
                        s_ref[b, c] = (s_ref[b, c] * w_t[:, lanes(c)] - skk * kapa_t[:, lanes(c)]
                                       + vb * km_t[:, lanes(c)])
                    else:
                        base = (bi * (HEAD_PAIRS // 2) + c // 2) * NH
                        ob = red[base:base + NH, (c % 2) * LANES:(c % 2 + 1) * LANES]
                    pltpu.store(ob_ref.at[b, c], ob, mask=omask)

    def step(t, carry):
        out_products(t - 1, True, t)
        return carry

    lax.fori_loop(0, tc, step, 0)
    out_products(tc - 1, False, None)
    st_ref[...] = s_ref[...]

    inv_n = 1.0 / RWKV_HEAD
    left = lane_i < NH
    for b in range(bb):
        rs = slice(b * tc, (b + 1) * tc)
        o_chunks = []
        for c in range(0, HEAD_PAIRS, 2):
            xt = jnp.concatenate([ob_ref[b, c], ob_ref[b, c + 1]], axis=0).T
            top, bot = xt[0:NH], xt[NH:2 * NH]
            o_chunks.append(jnp.where(left, top, pltpu.roll(bot, shift=NH, axis=1))[0:tc])
            o_chunks.append(jnp.where(left, pltpu.roll(top, shift=NH, axis=1), bot)[0:tc])
        o = jnp.concatenate(o_chunks, axis=1)
        mean = _head_sum(o, ones) * inv_n
        d = o - mean
        var = _head_sum(d * d, ones) * inv_n
        on = d * lax.rsqrt(var + GN_EPS) * lnw_ref[...] + lnb_ref[...]
        bonus = _head_sum(r_ref[rs, :] * km_ref[rs, :] * rk_ref[...], ones) * v_ref[rs, :]
        y_ref[b] = (on + bonus) * gates[b]


def _rwkv(p3, shift0, s0, mu, w0, ww, a0, wa, wg, kk, ka, rk, lnw, lnb, bb, tc):
    B, T, _ = p3.shape
    W = RWKV_WIDTH
    SW = 3 * W + LORA_PAD
    row = lambda: pl.BlockSpec((1, W), lambda b, t: (0, 0))
    lora = lambda: pl.BlockSpec((LORA_PAD, W), lambda b, t: (0, 0))
    chunk = lambda: pltpu.VMEM((bb * tc, W), F32)
    return pl.pallas_call(
        functools.partial(_rwkv_kernel, bb, tc),
        grid=(B // bb, T // tc),
        in_specs=[pl.BlockSpec((bb, tc, W), lambda b, t: (b, t, 2)),
                  pl.BlockSpec((bb, tc, W), lambda b, t: (b, t, 3)),
                  pl.BlockSpec((bb, tc, W), lambda b, t: (b, t, 4)),
                  pl.BlockSpec((bb, tc, LORA_PAD), lambda b, t: (b, t, LORA_COL_BLOCK)),
                  pl.BlockSpec((bb, 1, SW), lambda b, t: (b, 0, 0)),
                  pl.BlockSpec((bb, HEAD_PAIRS, RWKV_HEAD, LANES), lambda b, t: (b, 0, 0, 0)),
                  pl.BlockSpec((1, SW), lambda b, t: (0, 0)),
                  row(), lora(), row(), lora(), lora(), row(), row(), row(), row(), row()],
        out_specs=[pl.BlockSpec((bb, tc, W), lambda b, t: (b, t, 0)),
                   pl.BlockSpec((bb, HEAD_PAIRS, RWKV_HEAD, LANES), lambda b, t: (b, 0, 0, 0))],
        out_shape=[jax.ShapeDtypeStruct((B, T, W), F32),
                   jax.ShapeDtypeStruct((B, HEAD_PAIRS, RWKV_HEAD, LANES), F32)],
        scratch_shapes=[pltpu.VMEM((bb, HEAD_PAIRS, RWKV_HEAD, LANES), F32),
                        pltpu.VMEM((bb, 1, SW), F32),
                        chunk(), chunk(), chunk(), chunk(), chunk(), chunk(),
                        pltpu.VMEM((bb, HEAD_PAIRS, RWKV_HEAD, LANES), F32)],
        compiler_params=_cparams(("arbitrary", "arbitrary")),
        name="rwkv7",
    )(p3, p3, p3, p3, shift0, s0, mu, w0, ww, a0, wa, wg, kk, ka, rk, lnw, lnb)


def _pack_state(s):
    B = s.shape[0]
    s = s.reshape(B, HEAD_PAIRS, 2, RWKV_HEAD, RWKV_HEAD)
    return jnp.transpose(s, (0, 1, 3, 2, 4)).reshape(B, HEAD_PAIRS, RWKV_HEAD, LANES)


def _unpack_state(s):
    B = s.shape[0]
    s = s.reshape(B, HEAD_PAIRS, RWKV_HEAD, 2, RWKV_HEAD)
    return jnp.transpose(s, (0, 1, 3, 2, 4)).reshape(B, RWKV_HEADS, RWKV_HEAD, RWKV_HEAD)


def _pad_shift(a):
    pad = [(0, 0)] * (a.ndim - 1) + [(0, LORA_PAD - LORA_TOTAL)]
    return jnp.pad(a, pad)


def _lora_rows(w_up, offset):
    rank = w_up.shape[0]
    return jnp.pad(w_up, ((offset, LORA_PAD - offset - rank), (0, 0))).astype(BF16)


def _hybrid_layer(x, mem_k, mem_v, h0, conv0, S0, shift0, wts, cfg):
    B, T, D = x.shape
    M = B * T
    x2 = x.reshape(M, D)

    p = _matmul(x2, wts["w_in"], cfg["tm"], 512)
    p3 = p.reshape(B, T, P_PAD)

    y_lru, h_last, conv_new = _lru(p3, h0.reshape(B, 1, LRU_WIDTH), conv0,
                                   wts["conv_w"], wts["conv_b"], wts["lru_wa"], wts["lru_ba"],
                                   wts["lru_wx"], wts["lru_bx"], wts["lru_L"], cfg["lru_bb"], cfg["lru_tc"])

    y_rwkv, s_new = _rwkv(p3, _pad_shift(shift0).reshape(B, 1, -1), _pack_state(S0),
                          wts["mu"], wts["w0"], wts["ww"], wts["a0"], wts["wa"], wts["wg"],
                          wts["k_k"], wts["k_a"], wts["r_k"], wts["ln_w"], wts["ln_b"],
                          cfg["rwkv_bb"], cfg["rwkv_tc"])
    shift_new = p3[:, T - 1, 2 * LRU_WIDTH:P_TOTAL]

    x1 = _matmul_res_ln([y_lru.reshape(M, LRU_WIDTH), y_rwkv.reshape(M, RWKV_WIDTH)],
                        [wts["w_out_a"], wts["w_out_b"]], x2, wts["ln1_g"], wts["ln1_b"], cfg["tm_ln"])

    q = _matmul(x1, wts["wq"], cfg["tm"], 512)
    att = _attention(q.reshape(B, T, D), mem_k, mem_v, cfg["att_bb"], cfg["att_tq"])
    x2n = _matmul_res_ln([att.reshape(M, D)], [wts["wo"]], x1, wts["ln2_g"], wts["ln2_b"], cfg["tm_ln"])

    x3 = _mlp_ln(x2n, wts["w1"], wts["w2"], wts["ln3_g"], wts["ln3_b"], cfg["tm"], 512)
    return (x3.reshape(B, T, D), h_last.reshape(B, LRU_WIDTH), conv_new,
            _unpack_state(s_new), shift_new)


def kernel(x_prompt, x_sample, mem_prompt, cache_mem_k, cache_mem_v, state_lru_h, state_lru_conv, state_rwkv_S, state_rwkv_shift, w_in, lru_conv_w, lru_conv_b, lru_wa, lru_ba, lru_wx, lru_bx, lru_L, rwkv_mu, rwkv_w0, rwkv_w_up, rwkv_a0, rwkv_a_up, rwkv_g_up, rwkv_k_k, rwkv_k_a, rwkv_r_k, rwkv_ln_w, rwkv_ln_b, w_out, ln1_g, ln1_b, xa_wq, xa_wk, xa_wv, xa_wo, ln2_g, ln2_b, mlp_w1, mlp_w2, ln3_g, ln3_b):
    B, T, D = x_prompt.shape
    Bs, Ts, _ = x_sample.shape
    assert w_in.shape[0] == DEPTH

    cfg_p = dict(tm=512, tm_ln=256, lru_bb=1, lru_tc=256, rwkv_bb=B, rwkv_tc=64, att_bb=1, att_tq=512)
    cfg_s = dict(tm=512, tm_ln=256, lru_bb=8, lru_tc=Ts, rwkv_bb=4, rwkv_tc=Ts, att_bb=4, att_tq=Ts)

    yp, ys = x_prompt, x_sample
    outs_p = [[] for _ in range(6)]
    outs_s = [[] for _ in range(4)]
    for l in range(DEPTH):
        row = lambda a: a[l].reshape(1, -1)
        wts = dict(
            w_in=jnp.pad(w_in[l], ((0, 0), (0, P_PAD - P_TOTAL))).astype(BF16),
            conv_w=lru_conv_w[l], conv_b=row(lru_conv_b),
            lru_wa=lru_wa[l].astype(BF16), lru_ba=row(lru_ba),
            lru_wx=lru_wx[l].astype(BF16), lru_bx=row(lru_bx), lru_L=row(lru_L),
            mu=_pad_shift(rwkv_mu[l]).reshape(1, -1), w0=row(rwkv_w0), a0=row(rwkv_a0),
            ww=_lora_rows(rwkv_w_up[l], 0), wa=_lora_rows(rwkv_a_up[l], DECAY_LORA),
            wg=_lora_rows(rwkv_g_up[l], DECAY_LORA + A_LORA),
            k_k=row(rwkv_k_k), k_a=row(rwkv_k_a), r_k=row(rwkv_r_k), ln_w=row(rwkv_ln_w), ln_b=row(rwkv_ln_b),
            w_out_a=w_out[l, :LRU_WIDTH].astype(BF16), w_out_b=w_out[l, LRU_WIDTH:].astype(BF16),
            ln1_g=row(ln1_g), ln1_b=row(ln1_b),
            wq=xa_wq[l].astype(BF16), wo=xa_wo[l].astype(BF16), ln2_g=row(ln2_g), ln2_b=row(ln2_b),
            w1=mlp_w1[l].astype(BF16), w2=mlp_w2[l].astype(BF16), ln3_g=row(ln3_g), ln3_b=row(ln3_b),
        )
        mem2 = mem_prompt.reshape(B * N_MEM, D)
        mem_k = _matmul(mem2, xa_wk[l].astype(BF16), 512, 512).reshape(B, N_MEM, D)
        mem_v = _matmul(mem2, xa_wv[l].astype(BF16), 512, 512).reshape(B, N_MEM, D)
        yp, hl, cl, Sl, shl = _hybrid_layer(
            yp, mem_k, mem_v,
            jnp.zeros((B, LRU_WIDTH), F32), jnp.zeros((B, CONV_WIDTH - 1, LRU_WIDTH), F32),
            jnp.zeros((B, RWKV_HEADS, RWKV_HEAD, RWKV_HEAD), F32), jnp.zeros((B, SHIFT_WIDTH), F32),
            wts, cfg_p)
        for lst, val in zip(outs_p, (mem_k.reshape(B, N_MEM, X_HEADS, X_HEAD_DIM),
                                     mem_v.reshape(B, N_MEM, X_HEADS, X_HEAD_DIM), hl, cl, Sl, shl)):
            lst.append(val)
        ys, hl, cl, Sl, shl = _hybrid_layer(
            ys, cache_mem_k[l].reshape(Bs, N_MEM, D), cache_mem_v[l].reshape(Bs, N_MEM, D),
            state_lru_h[l], state_lru_conv[l], state_rwkv_S[l], state_rwkv_shift[l], wts, cfg_s)
        for lst, val in zip(outs_s, (hl, cl, Sl, shl)):
            lst.append(val)

    return (yp, ys, *(jnp.stack(o) for o in outs_p), *(jnp.stack(o) for o in outs_s))
```

```python
import functools
import math

import jax
import jax.numpy as jnp
from jax import lax
from jax.experimental import pallas as pl
from jax.experimental.pallas import tpu as pltpu

F32 = jnp.float32
BF16 = jnp.bfloat16

D_MODEL = 2048
LRU_WIDTH = 1024
LRU_BLOCKS = 8
LRU_BLOCK = 128
CONV_WIDTH = 4
LRU_C = 8.0
RWKV_WIDTH = 1024
RWKV_HEAD = 64
RWKV_HEADS = 16
DECAY_LORA = 64
A_LORA = 64
G_LORA = 160
LORA_TOTAL = DECAY_LORA + A_LORA + G_LORA
SHIFT_WIDTH = 3 * RWKV_WIDTH + LORA_TOTAL
P_TOTAL = 2 * LRU_WIDTH + SHIFT_WIDTH
DECAY_SCALE = math.exp(-0.5)
N_MEM = 256
X_HEADS = 4
X_HEAD_DIM = 512
D_FF = 4 * D_MODEL
LN_EPS = 1e-5
GN_EPS = 64e-5
DEPTH = 1
ALPHA = (2 * DEPTH) ** 0.25

LANES = 128
SUBLANES = 8
VMEM_LIMIT_BYTES = 56 * 1024 * 1024

LORA_PAD = 512
P_PAD = 5 * 1024 + LORA_PAD
LORA_COL_BLOCK = (5 * 1024) // LORA_PAD
HEAD_PAIRS = RWKV_WIDTH // LANES


def _cparams(sem):
    return pltpu.CompilerParams(dimension_semantics=sem, vmem_limit_bytes=VMEM_LIMIT_BYTES)


def _mm_kernel(x_ref, w_ref, o_ref, xb_ref):
    @pl.when(pl.program_id(1) == 0)
    def _():
        xb_ref[...] = x_ref[...].astype(BF16)

    o_ref[...] = jnp.dot(xb_ref[...], w_ref[...], preferred_element_type=F32).astype(o_ref.dtype)


def _matmul(x, w, tm, tn, out_dtype=F32):
    M, K = x.shape
    N = w.shape[1]
    assert M % tm == 0 and N % tn == 0
    return pl.pallas_call(
        _mm_kernel,
        grid=(M // tm, N // tn),
        in_specs=[pl.BlockSpec((tm, K), lambda i, j: (i, 0)),
                  pl.BlockSpec((K, tn), lambda i, j: (0, j))],
        out_specs=pl.BlockSpec((tm, tn), lambda i, j: (i, j)),
        out_shape=jax.ShapeDtypeStruct((M, N), out_dtype),
        scratch_shapes=[pltpu.VMEM((tm, K), BF16)],
        compiler_params=_cparams(("arbitrary", "arbitrary")),
        name="matmul",
    )(x, w)


def _layer_norm(y, g, b):
    mu = jnp.mean(y, axis=-1, keepdims=True)
    d = y - mu
    var = jnp.mean(d * d, axis=-1, keepdims=True)
    return d * lax.rsqrt(var + LN_EPS) * g + b


def _mm_ln_kernel(n_in, *refs):
    x_refs = refs[:n_in]
    w_refs = refs[n_in:2 * n_in]
    res_ref, g_ref, b_ref, o_ref = refs[2 * n_in:]
    acc = jnp.dot(x_refs[0][...].astype(BF16), w_refs[0][...], preferred_element_type=F32)
    for x_ref, w_ref in zip(x_refs[1:], w_refs[1:]):
        acc = acc + jnp.dot(x_ref[...].astype(BF16), w_ref[...], preferred_element_type=F32)
    y = ALPHA * res_ref[...] + acc
    o_ref[...] = _layer_norm(y, g_ref[...], b_ref[...])


def _matmul_res_ln(xs, ws, res, g, b, tm):
    M, N = res.shape
    n_in = len(xs)
    in_specs = [pl.BlockSpec((tm, x.shape[1]), lambda i: (i, 0)) for x in xs]
    in_specs += [pl.BlockSpec(w.shape, lambda i: (0, 0)) for w in ws]
    in_specs += [pl.BlockSpec((tm, N), lambda i: (i, 0)),
                 pl.BlockSpec((1, N), lambda i: (0, 0)),
                 pl.BlockSpec((1, N), lambda i: (0, 0))]
    return pl.pallas_call(
        functools.partial(_mm_ln_kernel, n_in),
        grid=(M // tm,),
        in_specs=in_specs,
        out_specs=pl.BlockSpec((tm, N), lambda i: (i, 0)),
        out_shape=jax.ShapeDtypeStruct((M, N), F32),
        compiler_params=_cparams(("arbitrary",)),
        name="matmul_res_ln",
    )(*xs, *ws, res, g, b)


def _mlp_kernel(x_ref, w1_ref, w2_ref, g_ref, b_ref, o_ref, xb_ref, acc_ref):
    f = pl.program_id(1)

    @pl.when(f == 0)
    def _():
        xb_ref[...] = x_ref[...].astype(BF16)
        acc_ref[...] = jnp.zeros_like(acc_ref)

    h = jnp.dot(xb_ref[...], w1_ref[...], preferred_element_type=F32)
    h = jnp.square(jnp.maximum(h, 0.0))
    acc_ref[...] += jnp.dot(h.astype(BF16), w2_ref[...], preferred_element_type=F32)

    @pl.when(f == pl.num_programs(1) - 1)
    def _():
        y = ALPHA * x_ref[...] + acc_ref[...]
        o_ref[...] = _layer_norm(y, g_ref[...], b_ref[...])


def _mlp_ln(x, w1, w2, g, b, tm, tf):
    M, D = x.shape
    FF = w1.shape[1]
    return pl.pallas_call(
        _mlp_kernel,
        grid=(M // tm, FF // tf),
        in_specs=[pl.BlockSpec((tm, D), lambda i, f: (i, 0)),
                  pl.BlockSpec((D, tf), lambda i, f: (0, f)),
                  pl.BlockSpec((tf, D), lambda i, f: (f, 0)),
                  pl.BlockSpec((1, D), lambda i, f: (0, 0)),
                  pl.BlockSpec((1, D), lambda i, f: (0, 0))],
        out_specs=pl.BlockSpec((tm, D), lambda i, f: (i, 0)),
        out_shape=jax.ShapeDtypeStruct((M, D), F32),
        scratch_shapes=[pltpu.VMEM((tm, D), BF16), pltpu.VMEM((tm, D), F32)],
        compiler_params=_cparams(("arbitrary", "arbitrary")),
        name="mlp_ln",
    )(x, w1, w2, g, b)


def _attn_kernel(bb, q_ref, k_ref, v_ref, o_ref):
    scale = X_HEAD_DIM ** -0.5
    for b in range(bb):
        q = q_ref[b].astype(BF16)
        k = k_ref[b].astype(BF16)
        v = v_ref[b].astype(BF16)
        s = lax.dot_general(q, k, (((1,), (1,)), ((), ())), preferred_element_type=F32) * scale
        m = jnp.max(s, axis=-1, keepdims=True)
        e = jnp.exp(s - m)
        p = e / jnp.sum(e, axis=-1, keepdims=True)
        o_ref[b] = jnp.dot(p.astype(BF16), v, preferred_element_type=F32)


def _softmax_pv(s, v):
    m = jnp.max(s, axis=-1, keepdims=True)
    e = jnp.exp(s - m)
    p = e / jnp.sum(e, axis=-1, keepdims=True)
    return jnp.dot(p.astype(BF16), v, preferred_element_type=F32)


def _attn_cache_kernel(bb, q_ref, k_ref, v_ref, o_ref):
    scale = X_HEAD_DIM ** -0.5
    T = q_ref.shape[1]
    rows = N_MEM * X_HEADS
    q_head = lax.broadcasted_iota(jnp.int32, (X_HEADS * T, rows), 0) // T
    k_head = lax.broadcasted_iota(jnp.int32, (X_HEADS * T, rows), 1) & (X_HEADS - 1)
    own_head = q_head == k_head
    for b in range(bb):
        q = jnp.concatenate([q_ref[b, :, h * X_HEAD_DIM:(h + 1) * X_HEAD_DIM] for h in range(X_HEADS)], axis=0)
        k = k_ref[b].reshape(rows, X_HEAD_DIM).astype(BF16)
        v = v_ref[b].reshape(rows, X_HEAD_DIM).astype(BF16)
        s = lax.dot_general(q.astype(BF16), k, (((1,), (1,)), ((), ())), preferred_element_type=F32) * scale
        o = _softmax_pv(jnp.where(own_head, s, -jnp.inf), v)
        for h in range(X_HEADS):
            o_ref[b, :, h * X_HEAD_DIM:(h + 1) * X_HEAD_DIM] = o[h * T:(h + 1) * T]


def _attention_cache(q, cache_k, cache_v, bb):
    B, T, D = q.shape
    kv_spec = lambda: pl.BlockSpec((bb, N_MEM, X_HEADS, X_HEAD_DIM), lambda b: (b, 0, 0, 0))
    return pl.pallas_call(
        functools.partial(_attn_cache_kernel, bb),
        grid=(B // bb,),
        in_specs=[pl.BlockSpec((bb, T, D), lambda b: (b, 0, 0)), kv_spec(), kv_spec()],
        out_specs=pl.BlockSpec((bb, T, D), lambda b: (b, 0, 0)),
        out_shape=jax.ShapeDtypeStruct((B, T, D), F32),
        compiler_params=_cparams(("arbitrary",)),
        name="cache_attention",
    )(q, cache_k, cache_v)


def _attention(q, mem_k, mem_v, bb, tq):
    B, T, D = q.shape
    return pl.pallas_call(
        functools.partial(_attn_kernel, bb),
        grid=(B // bb, X_HEADS, T // tq),
        in_specs=[pl.BlockSpec((bb, tq, X_HEAD_DIM), lambda b, h, t: (b, t, h)),
                  pl.BlockSpec((bb, N_MEM, X_HEAD_DIM), lambda b, h, t: (b, 0, h)),
                  pl.BlockSpec((bb, N_MEM, X_HEAD_DIM), lambda b, h, t: (b, 0, h))],
        out_specs=pl.BlockSpec((bb, tq, X_HEAD_DIM), lambda b, h, t: (b, t, h)),
        out_shape=jax.ShapeDtypeStruct((B, T, D), F32),
        compiler_params=_cparams(("arbitrary", "arbitrary", "arbitrary")),
        name="mem_attention",
    )(q, mem_k, mem_v)


def _softplus(x):
    return jnp.maximum(x, 0.0) + jnp.log1p(jnp.exp(-jnp.abs(x)))


def _lru_kernel(bb, tc, ux_ref, ug_ref, h0_ref, c0_ref, cw_ref, cb_ref, wa_ref, ba_ref, wx_ref, bx_ref,
                lam_ref, y_ref, hl_ref, cn_ref, ext_ref, hc_ref, a_ref, b_ref, hs_ref):
    ti = pl.program_id(1)
    pad = SUBLANES
    neg_c_sp = -LRU_C * _softplus(-lam_ref[...])

    for b in range(bb):
        @pl.when(ti == 0)
        def _():
            ext_ref[b, pad - 3:pad, :] = c0_ref[b]
            hc_ref[b] = h0_ref[b]

        ext_ref[b, pad:pad + tc, :] = ux_ref[b]
        xc = cb_ref[...] + cw_ref[0:1, :] * ext_ref[b, pad - 3:pad - 3 + tc, :]
        for j in range(1, CONV_WIDTH):
            xc = xc + cw_ref[j:j + 1, :] * ext_ref[b, pad - 3 + j:pad - 3 + j + tc, :]

        ga, gx = [], []
        for n in range(LRU_BLOCKS):
            xb = xc[:, n * LRU_BLOCK:(n + 1) * LRU_BLOCK].astype(BF16)
            ga.append(jnp.dot(xb, wa_ref[n], preferred_element_type=F32))
            gx.append(jnp.dot(xb, wx_ref[n], preferred_element_type=F32))
        r = jax.nn.sigmoid(jnp.concatenate(ga, axis=1) + ba_ref[...])
        i = jax.nn.sigmoid(jnp.concatenate(gx, axis=1) + bx_ref[...])
        log_a = neg_c_sp * r
        a_ref[...] = jnp.exp(log_a)
        b_ref[...] = jnp.sqrt(1.0 - jnp.exp(2.0 * log_a)) * (i * xc)

        def step(t, h):
            h = a_ref[pl.ds(t, 1), :] * h + b_ref[pl.ds(t, 1), :]
            hs_ref[pl.ds(t, 1), :] = h
            return h

        h = lax.fori_loop(0, tc, step, hc_ref[b])
        hc_ref[b] = h
        hl_ref[b] = h
        y_ref[b] = hs_ref[...] * jax.nn.gelu(ug_ref[b])
        tail = ext_ref[b, pad + tc - 3:pad + tc, :]
        cn_ref[b] = tail
        ext_ref[b, pad - 3:pad, :] = tail


def _lru(p3, h0, conv0, cw, cb, wa, ba, wx, bx, lam, bb, tc):
    B, T, _ = p3.shape
    W = LRU_WIDTH
    row = lambda: pl.BlockSpec((1, W), lambda b, t: (0, 0))
    blk = lambda: pl.BlockSpec((LRU_BLOCKS, LRU_BLOCK, LRU_BLOCK), lambda b, t: (0, 0, 0))
    return pl.pallas_call(
        functools.partial(_lru_kernel, bb, tc),
        grid=(B // bb, T // tc),
        in_specs=[pl.BlockSpec((bb, tc, W), lambda b, t: (b, t, 0)),
                  pl.BlockSpec((bb, tc, W), lambda b, t: (b, t, 1)),
                  pl.BlockSpec((bb, 1, W), lambda b, t: (b, 0, 0)),
                  pl.BlockSpec((bb, CONV_WIDTH - 1, W), lambda b, t: (b, 0, 0)),
                  pl.BlockSpec((CONV_WIDTH, W), lambda b, t: (0, 0)),
                  row(), blk(), row(), blk(), row(), row()],
        out_specs=[pl.BlockSpec((bb, tc, W), lambda b, t: (b, t, 0)),
                   pl.BlockSpec((bb, 1, W), lambda b, t: (b, 0, 0)),
                   pl.BlockSpec((bb, CONV_WIDTH - 1, W), lambda b, t: (b, 0, 0))],
        out_shape=[jax.ShapeDtypeStruct((B, T, W), F32),
                   jax.ShapeDtypeStruct((B, 1, W), F32),
                   jax.ShapeDtypeStruct((B, CONV_WIDTH - 1, W), F32)],
        scratch_shapes=[pltpu.VMEM((bb, tc + SUBLANES, W), F32),
                        pltpu.VMEM((bb, 1, W), F32),
                        pltpu.VMEM((tc, W), F32),
                        pltpu.VMEM((tc, W), F32),
                        pltpu.VMEM((tc, W), F32)],
        compiler_params=_cparams(("arbitrary", "arbitrary")),
        name="rg_lru",
    )(p3, p3, h0, conv0, cw, cb, wa, ba, wx, bx, lam)


def _head_ones(n):
    ri = lax.broadcasted_iota(jnp.int32, (n, n), 0)
    ci = lax.broadcasted_iota(jnp.int32, (n, n), 1)
    return ((ri >> 6) == (ci >> 6)).astype(BF16)


def _head_sum(x, ones):
    outs = []
    for c in range(HEAD_PAIRS):
        xc = x[:, c * LANES:(c + 1) * LANES]
        hi = xc.astype(BF16)
        r1 = xc - hi.astype(F32)
        mid = r1.astype(BF16)
        lo = (r1 - mid.astype(F32)).astype(BF16)
        s = jnp.dot(hi, ones, preferred_element_type=F32)
        s = s + jnp.dot(mid, ones, preferred_element_type=F32)
        s = s + jnp.dot(lo, ones, preferred_element_type=F32)
        outs.append(s)
    return jnp.concatenate(outs, axis=1)


def _rwkv_kernel(bb, tc, ur_ref, uk_ref, uv_ref, ul_ref, sh0_ref, s0_ref, mu_ref, w0_ref, ww_ref, a0_ref,
                 wa_ref, wg_ref, kk_ref, ka_ref, rk_ref, lnw_ref, lnb_ref,
                 y_ref, st_ref,
                 s_ref, carry_ref, kap_ref, kapa_ref, w_ref, km_ref, v_ref, r_ref, ob_ref):
    ti = pl.program_id(1)
    ones = _head_ones(LANES)
    ones2 = _head_ones(2 * LANES)
    W = RWKV_WIDTH
    NH = RWKV_HEAD
    row0 = lax.broadcasted_iota(jnp.int32, (tc, 1), 0) == 0
    sub_i = lax.broadcasted_iota(jnp.int32, (NH, LANES), 0)
    lane_i = lax.broadcasted_iota(jnp.int32, (NH, LANES), 1)
    lane_t = lane_i & (NH - 1)
    eye16 = (lane_t == sub_i).astype(BF16)
    gb = 2 if bb % 2 == 0 else 1

    @pl.when(ti == 0)
    def _():
        s_ref[...] = s0_ref[...]
        carry_ref[...] = sh0_ref[...]
        ob_ref[...] = jnp.zeros_like(ob_ref)

    def shifted(u, b, lo, width):
        prev = jnp.where(row0, carry_ref[b, :, lo:lo + width], pltpu.roll(u, shift=1, axis=0))
        return u + (prev - u) * mu_ref[:, lo:lo + width]

    gates = []
    for b in range(bb):
        ur, uk, uv, ul = ur_ref[b], uk_ref[b], uv_ref[b], ul_ref[b]
        zr = shifted(ur, b, 0, W)
        zk = shifted(uk, b, W, W)
        zv = shifted(uv, b, 2 * W, W)
        zl = shifted(ul, b, 3 * W, LORA_PAD)
        carry_ref[b, :, 0:W] = ur[tc - 1:tc, :]
        carry_ref[b, :, W:2 * W] = uk[tc - 1:tc, :]
        carry_ref[b, :, 2 * W:3 * W] = uv[tc - 1:tc, :]
        carry_ref[b, :, 3 * W:3 * W + LORA_PAD] = ul[tc - 1:tc, :]

        dw = jnp.dot(jnp.tanh(zl).astype(BF16), ww_ref[...], preferred_element_type=F32)
        da = jnp.dot(zl.astype(BF16), wa_ref[...], preferred_element_type=F32)
        g = jnp.dot(jax.nn.sigmoid(zl).astype(BF16), wg_ref[...], preferred_element_type=F32)
        w = jnp.exp(-DECAY_SCALE * jax.nn.sigmoid(w0_ref[...] + dw))
        a = jax.nn.sigmoid(a0_ref[...] + da)
        kk = zk * kk_ref[...]
        kap = kk * lax.rsqrt(jnp.maximum(_head_sum(kk * kk, ones), 1e-24))
        km = zk * (1.0 + (a - 1.0) * ka_ref[...])
        rs = slice(b * tc, (b + 1) * tc)
        kap_ref[rs, :] = kap
        kapa_ref[rs, :] = kap * a
        w_ref[rs, :] = w
        km_ref[rs, :] = km
        v_ref[rs, :] = zv
        r_ref[rs, :] = zr
        gates.append(g)

    def lanes(c):
        return slice(c * LANES, (c + 1) * LANES)

    def out_products(t_out, with_update, t):
        omask = lane_t == t_out
        for g0 in range(0, bb, gb):
            group = range(g0, g0 + gb)
            lhs, rows = [], []
            for b in group:
                r_p = r_ref[pl.ds(b * tc + jnp.maximum(t_out, 0), 1), :]
                if with_update:
                    row = pl.ds(b * tc + t, 1)
                    kap_t, v_t = kap_ref[row, :], v_ref[row, :]
                    rows.append((kapa_ref[row, :], w_ref[row, :], km_ref[row, :]))
                    for c0 in range(0, HEAD_PAIRS, 2):
                        dv = []
                        for c in (c0, c0 + 1):
                            s = s_ref[b, c]
                            lhs.append(jnp.concatenate([(s * kap_t[:, lanes(c)]).astype(BF16),
                                                        (s * r_p[:, lanes(c)]).astype(BF16)], axis=1))
                            dv.append(eye16 * jnp.broadcast_to(v_t[:, lanes(c)], (NH, LANES)).astype(BF16))
                        lhs.append(jnp.concatenate(dv, axis=1))
                else:
                    for c in range(0, HEAD_PAIRS, 2):
                        lhs.append(jnp.concatenate([(s_ref[b, c] * r_p[:, lanes(c)]).astype(BF16),
                                                    (s_ref[b, c + 1] * r_p[:, lanes(c + 1)]).astype(BF16)], axis=1))
            red = jnp.dot(jnp.concatenate(lhs, axis=0), ones2, preferred_element_type=F32)
            for bi, b in enumerate(group):
                for c in range(HEAD_PAIRS):
                    if with_update:
                        base = ((bi * (HEAD_PAIRS // 2) + c // 2) * 3 + c % 2) * NH
                        skk = red[base:base + NH, 0:LANES]
                        ob = red[base:base + NH, LANES:2 * LANES]
                        vbase = ((bi * (HEAD_PAIRS // 2) + c // 2) * 3 + 2) * NH
                        vb = red[vbase:vbase + NH, (c % 2) * LANES:(c % 2 + 1) * LANES]
                        kapa_t, w_t, km_t = rows[bi]
                        s_ref[b, c] = (s_ref[b, c] * w_t[:, lanes(c)] - skk * kapa_t[:, lanes(c)]
                                       + vb * km_t[:, lanes(c)])
                    else:
                        base = (bi * (HEAD_PAIRS // 2) + c // 2) * NH
                        ob = red[base:base + NH, (c % 2) * LANES:(c % 2 + 1) * LANES]
                    pltpu.store(ob_ref.at[b, c], ob, mask=omask)

    def step(t, carry):
        out_products(t - 1, True, t)
        return carry

    lax.fori_loop(0, tc, step, 0, unroll=2)
    out_products(tc - 1, False, None)
    st_ref[...] = s_ref[...]

    inv_n = 1.0 / RWKV_HEAD
    left = lane_i < NH
    for b in range(bb):
        rs = slice(b * tc, (b + 1) * tc)
        o_chunks = []
        for c in range(0, HEAD_PAIRS, 2):
            xt = jnp.concatenate([ob_ref[b, c], ob_ref[b, c + 1]], axis=0).T
            top, bot = xt[0:NH], xt[NH:2 * NH]
            o_chunks.append(jnp.where(left, top, pltpu.roll(bot, shift=NH, axis=1))[0:tc])
            o_chunks.append(jnp.where(left, pltpu.roll(top, shift=NH, axis=1), bot)[0:tc])
        o = jnp.concatenate(o_chunks, axis=1)
        mean = _head_sum(o, ones) * inv_n
        d = o - mean
        var = _head_sum(d * d, ones) * inv_n
        on = d * lax.rsqrt(var + GN_EPS) * lnw_ref[...] + lnb_ref[...]
        bonus = _head_sum(r_ref[rs, :] * km_ref[rs, :] * rk_ref[...], ones) * v_ref[rs, :]
        y_ref[b] = (on + bonus) * gates[b]


def _rwkv(p3, shift0, s0, mu, w0, ww, a0, wa, wg, kk, ka, rk, lnw, lnb, bb, tc):
    B, T, _ = p3.shape
    W = RWKV_WIDTH
    SW = 3 * W + LORA_PAD
    row = lambda: pl.BlockSpec((1, W), lambda b, t: (0, 0))
    lora = lambda: pl.BlockSpec((LORA_PAD, W), lambda b, t: (0, 0))
    chunk = lambda: pltpu.VMEM((bb * tc, W), F32)
    return pl.pallas_call(
        functools.partial(_rwkv_kernel, bb, tc),
        grid=(B // bb, T // tc),
        in_specs=[pl.BlockSpec((bb, tc, W), lambda b, t: (b, t, 2)),
                  pl.BlockSpec((bb, tc, W), lambda b, t: (b, t, 3)),
                  pl.BlockSpec((bb, tc, W), lambda b, t: (b, t, 4)),
                  pl.BlockSpec((bb, tc, LORA_PAD), lambda b, t: (b, t, LORA_COL_BLOCK)),
                  pl.BlockSpec((bb, 1, SW), lambda b, t: (b, 0, 0)),
                  pl.BlockSpec((bb, HEAD_PAIRS, RWKV_HEAD, LANES), lambda b, t: (b, 0, 0, 0)),
                  pl.BlockSpec((1, SW), lambda b, t: (0, 0)),
                  row(), lora(), row(), lora(), lora(), row(), row(), row(), row(), row()],
        out_specs=[pl.BlockSpec((bb, tc, W), lambda b, t: (b, t, 0)),
                   pl.BlockSpec((bb, HEAD_PAIRS, RWKV_HEAD, LANES), lambda b, t: (b, 0, 0, 0))],
        out_shape=[jax.ShapeDtypeStruct((B, T, W), F32),
                   jax.ShapeDtypeStruct((B, HEAD_PAIRS, RWKV_HEAD, LANES), F32)],
        scratch_shapes=[pltpu.VMEM((bb, HEAD_PAIRS, RWKV_HEAD, LANES), F32),
                        pltpu.VMEM((bb, 1, SW), F32),
                        chunk(), chunk(), chunk(), chunk(), chunk(), chunk(),
                        pltpu.VMEM((bb, HEAD_PAIRS, RWKV_HEAD, LANES), F32)],
        compiler_params=_cparams(("arbitrary", "arbitrary")),
        name="rwkv7",
    )(p3, p3, p3, p3, shift0, s0, mu, w0, ww, a0, wa, wg, kk, ka, rk, lnw, lnb)


def _pack_state(s):
    B = s.shape[0]
    s = s.reshape(B, HEAD_PAIRS, 2, RWKV_HEAD, RWKV_HEAD)
    return jnp.transpose(s, (0, 1, 3, 2, 4)).reshape(B, HEAD_PAIRS, RWKV_HEAD, LANES)


def _unpack_state(s):
    B = s.shape[0]
    s = s.reshape(B, HEAD_PAIRS, RWKV_HEAD, 2, RWKV_HEAD)
    return jnp.transpose(s, (0, 1, 3, 2, 4)).reshape(B, RWKV_HEADS, RWKV_HEAD, RWKV_HEAD)


def _pad_shift(a):
    pad = [(0, 0)] * (a.ndim - 1) + [(0, LORA_PAD - LORA_TOTAL)]
    return jnp.pad(a, pad)


def _lora_rows(w_up, offset):
    rank = w_up.shape[0]
    return jnp.pad(w_up, ((offset, LORA_PAD - offset - rank), (0, 0))).astype(BF16)


def _hybrid_layer(x, mem_k, mem_v, h0, conv0, S0, shift0, wts, cfg):
    B, T, D = x.shape
    M = B * T
    x2 = x.reshape(M, D)

    p = _matmul(x2, wts["w_in"], cfg["tm_mm"], 512)
    p3 = p.reshape(B, T, P_PAD)

    y_lru, h_last, conv_new = _lru(p3, h0.reshape(B, 1, LRU_WIDTH), conv0,
                                   wts["conv_w"], wts["conv_b"], wts["lru_wa"], wts["lru_ba"],
                                   wts["lru_wx"], wts["lru_bx"], wts["lru_L"], cfg["lru_bb"], cfg["lru_tc"])

    y_rwkv, s_new = _rwkv(p3, _pad_shift(shift0).reshape(B, 1, -1), _pack_state(S0),
                          wts["mu"], wts["w0"], wts["ww"], wts["a0"], wts["wa"], wts["wg"],
                          wts["k_k"], wts["k_a"], wts["r_k"], wts["ln_w"], wts["ln_b"],
                          cfg["rwkv_bb"], cfg["rwkv_tc"])
    shift_new = p3[:, T - 1, 2 * LRU_WIDTH:P_TOTAL]

    x1 = _matmul_res_ln([y_lru.reshape(M, LRU_WIDTH), y_rwkv.reshape(M, RWKV_WIDTH)],
                        [wts["w_out_a"], wts["w_out_b"]], x2, wts["ln1_g"], wts["ln1_b"], cfg["tm_ln"])

    q = _matmul(x1, wts["wq"], cfg["tm_mm"], 512)
    if mem_k.ndim == 4:
        att = _attention_cache(q.reshape(B, T, D), mem_k, mem_v, cfg["att_bb"])
    else:
        att = _attention(q.reshape(B, T, D), mem_k, mem_v, cfg["att_bb"], cfg["att_tq"])
    x2n = _matmul_res_ln([att.reshape(M, D)], [wts["wo"]], x1, wts["ln2_g"], wts["ln2_b"], cfg["tm_ln"])

    x3 = _mlp_ln(x2n, wts["w1"], wts["w2"], wts["ln3_g"], wts["ln3_b"], cfg["tm"], cfg["tf"])
    return (x3.reshape(B, T, D), h_last.reshape(B, LRU_WIDTH), conv_new,
            _unpack_state(s_new), shift_new)


def kernel(x_prompt, x_sample, mem_prompt, cache_mem_k, cache_mem_v, state_lru_h, state_lru_conv, state_rwkv_S, state_rwkv_shift, w_in, lru_conv_w, lru_conv_b, lru_wa, lru_ba, lru_wx, lru_bx, lru_L, rwkv_mu, rwkv_w0, rwkv_w_up, rwkv_a0, rwkv_a_up, rwkv_g_up, rwkv_k_k, rwkv_k_a, rwkv_r_k, rwkv_ln_w, rwkv_ln_b, w_out, ln1_g, ln1_b, xa_wq, xa_wk, xa_wv, xa_wo, ln2_g, ln2_b, mlp_w1, mlp_w2, ln3_g, ln3_b):
    B, T, D = x_prompt.shape
    Bs, Ts, _ = x_sample.shape
    assert w_in.shape[0] == DEPTH

    cfg_p = dict(tm_mm=1024, tm=512, tf=1024, tm_ln=256, lru_bb=1, lru_tc=256, rwkv_bb=B, rwkv_tc=64,
                 att_bb=1, att_tq=512)
    cfg_s = dict(tm_mm=1024, tm=512, tf=1024, tm_ln=256, lru_bb=8, lru_tc=Ts, rwkv_bb=4, rwkv_tc=Ts,
                 att_bb=2, att_tq=Ts)

    yp, ys = x_prompt, x_sample
    outs_p = [[] for _ in range(6)]
    outs_s = [[] for _ in range(4)]
    for l in range(DEPTH):
        row = lambda a: a[l].reshape(1, -1)
        wts = dict(
            w_in=jnp.pad(w_in[l], ((0, 0), (0, P_PAD - P_TOTAL))).astype(BF16),
            conv_w=lru_conv_w[l], conv_b=row(lru_conv_b),
            lru_wa=lru_wa[l].astype(BF16), lru_ba=row(lru_ba),
            lru_wx=lru_wx[l].astype(BF16), lru_bx=row(lru_bx), lru_L=row(lru_L),
            mu=_pad_shift(rwkv_mu[l]).reshape(1, -1), w0=row(rwkv_w0), a0=row(rwkv_a0),
            ww=_lora_rows(rwkv_w_up[l], 0), wa=_lora_rows(rwkv_a_up[l], DECAY_LORA),
            wg=_lora_rows(rwkv_g_up[l], DECAY_LORA + A_LORA),
            k_k=row(rwkv_k_k), k_a=row(rwkv_k_a), r_k=row(rwkv_r_k), ln_w=row(rwkv_ln_w), ln_b=row(rwkv_ln_b),
            w_out_a=w_out[l, :LRU_WIDTH].astype(BF16), w_out_b=w_out[l, LRU_WIDTH:].astype(BF16),
            ln1_g=row(ln1_g), ln1_b=row(ln1_b),
            wq=xa_wq[l].astype(BF16), wo=xa_wo[l].astype(BF16), ln2_g=row(ln2_g), ln2_b=row(ln2_b),
            w1=mlp_w1[l].astype(BF16), w2=mlp_w2[l].astype(BF16), ln3_g=row(ln3_g), ln3_b=row(ln3_b),
        )
        mem2 = mem_prompt.reshape(B * N_MEM, D)
        mem_k = _matmul(mem2, xa_wk[l].astype(BF16), 512, 512).reshape(B, N_MEM, D)
        mem_v = _matmul(mem2, xa_wv[l].astype(BF16), 512, 512).reshape(B, N_MEM, D)
        yp, hl, cl, Sl, shl = _hybrid_layer(
            yp, mem_k, mem_v,
            jnp.zeros((B, LRU_WIDTH), F32), jnp.zeros((B, CONV_WIDTH - 1, LRU_WIDTH), F32),
            jnp.zeros((B, RWKV_HEADS, RWKV_HEAD, RWKV_HEAD), F32), jnp.zeros((B, SHIFT_WIDTH), F32),
            wts, cfg_p)
        for lst, val in zip(outs_p, (mem_k.reshape(B, N_MEM, X_HEADS, X_HEAD_DIM),
                                     mem_v.reshape(B, N_MEM, X_HEADS, X_HEAD_DIM), hl, cl, Sl, shl)):
            lst.append(val)
        ys, hl, cl, Sl, shl = _hybrid_layer(
            ys, cache_mem_k[l], cache_mem_v[l],
            state_lru_h[l], state_lru_conv[l], state_rwkv_S[l], state_rwkv_shift[l], wts, cfg_s)
        for lst, val in zip(outs_s, (hl, cl, Sl, shl)):
            lst.append(val)

    return (yp, ys, *(jnp.stack(o) for o in outs_p), *(jnp.stack(o) for o in outs_s))
```

```python
import functools
import math

import jax
import jax.numpy as jnp
from jax import lax
from jax.experimental import pallas as pl
from jax.experimental.pallas import tpu as pltpu

F32 = jnp.float32
BF16 = jnp.bfloat16

D_MODEL = 2048
LRU_WIDTH = 1024
LRU_BLOCKS = 8
LRU_BLOCK = 128
CONV_WIDTH = 4
LRU_C = 8.0
RWKV_WIDTH = 1024
RWKV_HEAD = 64
RWKV_HEADS = 16
DECAY_LORA = 64
A_LORA = 64
G_LORA = 160
LORA_TOTAL = DECAY_LORA + A_LORA + G_LORA
SHIFT_WIDTH = 3 * RWKV_WIDTH + LORA_TOTAL
P_TOTAL = 2 * LRU_WIDTH + SHIFT_WIDTH
DECAY_SCALE = math.exp(-0.5)
N_MEM = 256
X_HEADS = 4
X_HEAD_DIM = 512
D_FF = 4 * D_MODEL
LN_EPS = 1e-5
GN_EPS = 64e-5
DEPTH = 1
ALPHA = (2 * DEPTH) ** 0.25

LANES = 128
SUBLANES = 8
VMEM_LIMIT_BYTES = 56 * 1024 * 1024

LORA_PAD = 512
LORA_WA_PAD = DECAY_LORA + A_LORA
LORA_G_PAD = 2 * LANES
P_PAD = 5 * 1024 + LORA_PAD
LORA_COL_BLOCK = (5 * 1024) // LORA_PAD
HEAD_PAIRS = RWKV_WIDTH // LANES


def _cparams(sem):
    return pltpu.CompilerParams(dimension_semantics=sem, vmem_limit_bytes=VMEM_LIMIT_BYTES)


def _mm_kernel(n_main, has_tail, x_ref, w_ref, *rest):
    if has_tail:
        wt_ref, o_ref, xb_ref = rest
    else:
        o_ref, xb_ref = rest
    j = pl.program_id(1)

    @pl.when(j == 0)
    def _():
        xb_ref[...] = x_ref[...].astype(BF16)

    @pl.when(j < n_main)
    def _():
        o_ref[...] = jnp.dot(xb_ref[...], w_ref[...].astype(BF16), preferred_element_type=F32)

    if has_tail:
        @pl.when(j == n_main)
        def _():
            o_ref[...] = jnp.dot(xb_ref[...], wt_ref[...], preferred_element_type=F32)


def _matmul(x, w, tm, tn, w_tail=None):
    M, K = x.shape
    n_main = w.shape[1] // tn
    has_tail = w_tail is not None
    n_tiles = n_main + int(has_tail)
    assert M % tm == 0 and (has_tail or w.shape[1] % tn == 0)
    in_specs = [pl.BlockSpec((tm, K), lambda i, j: (i, 0)),
                pl.BlockSpec((K, tn), lambda i, j: (0, jnp.minimum(j, n_main - 1)))]
    args = [x, w]
    if has_tail:
        in_specs.append(pl.BlockSpec((K, tn), lambda i, j: (0, 0)))
        args.append(w_tail)
    return pl.pallas_call(
        functools.partial(_mm_kernel, n_main, has_tail),
        grid=(M // tm, n_tiles),
        in_specs=in_specs,
        out_specs=pl.BlockSpec((tm, tn), lambda i, j: (i, j)),
        out_shape=jax.ShapeDtypeStruct((M, n_tiles * tn), F32),
        scratch_shapes=[pltpu.VMEM((tm, K), BF16)],
        compiler_params=_cparams(("arbitrary", "arbitrary")),
        name="matmul",
    )(*args)


def _layer_norm(y, g, b):
    mu = jnp.mean(y, axis=-1, keepdims=True)
    d = y - mu
    var = jnp.mean(d * d, axis=-1, keepdims=True)
    return d * lax.rsqrt(var + LN_EPS) * g + b


def _mm_ln_kernel(n_in, *refs):
    x_refs = refs[:n_in]
    w_refs = refs[n_in:2 * n_in]
    res_ref, g_ref, b_ref, o_ref = refs[2 * n_in:]
    acc = jnp.dot(x_refs[0][...].astype(BF16), w_refs[0][...], preferred_element_type=F32)
    for x_ref, w_ref in zip(x_refs[1:], w_refs[1:]):
        acc = acc + jnp.dot(x_ref[...].astype(BF16), w_ref[...], preferred_element_type=F32)
    y = ALPHA * res_ref[...] + acc
    o_ref[...] = _layer_norm(y, g_ref[...], b_ref[...])


def _matmul_res_ln(xs, ws, res, g, b, tm):
    M, N = res.shape
    n_in = len(xs)
    in_specs = [pl.BlockSpec((tm, x.shape[1]), lambda i: (i, 0)) for x in xs]
    in_specs += [pl.BlockSpec(w.shape, lambda i: (0, 0)) for w in ws]
    in_specs += [pl.BlockSpec((tm, N), lambda i: (i, 0)),
                 pl.BlockSpec((1, N), lambda i: (0, 0)),
                 pl.BlockSpec((1, N), lambda i: (0, 0))]
    return pl.pallas_call(
        functools.partial(_mm_ln_kernel, n_in),
        grid=(M // tm,),
        in_specs=in_specs,
        out_specs=pl.BlockSpec((tm, N), lambda i: (i, 0)),
        out_shape=jax.ShapeDtypeStruct((M, N), F32),
        compiler_params=_cparams(("arbitrary",)),
        name="matmul_res_ln",
    )(*xs, *ws, res, g, b)


def _mlp_kernel(x_ref, w1_ref, w2_ref, g_ref, b_ref, o_ref, xb_ref, acc_ref):
    f = pl.program_id(1)

    @pl.when(f == 0)
    def _():
        xb_ref[...] = x_ref[...].astype(BF16)
        acc_ref[...] = jnp.zeros_like(acc_ref)

    h = jnp.dot(xb_ref[...], w1_ref[...], preferred_element_type=F32)
    h = jnp.square(jnp.maximum(h, 0.0))
    acc_ref[...] += jnp.dot(h.astype(BF16), w2_ref[...], preferred_element_type=F32)

    @pl.when(f == pl.num_programs(1) - 1)
    def _():
        y = ALPHA * x_ref[...] + acc_ref[...]
        o_ref[...] = _layer_norm(y, g_ref[...], b_ref[...])


def _mlp_ln(x, w1, w2, g, b, tm, tf):
    M, D = x.shape
    FF = w1.shape[1]
    return pl.pallas_call(
        _mlp_kernel,
        grid=(M // tm, FF // tf),
        in_specs=[pl.BlockSpec((tm, D), lambda i, f: (i, 0)),
                  pl.BlockSpec((D, tf), lambda i, f: (0, f)),
                  pl.BlockSpec((tf, D), lambda i, f: (f, 0)),
                  pl.BlockSpec((1, D), lambda i, f: (0, 0)),
                  pl.BlockSpec((1, D), lambda i, f: (0, 0))],
        out_specs=pl.BlockSpec((tm, D), lambda i, f: (i, 0)),
        out_shape=jax.ShapeDtypeStruct((M, D), F32),
        scratch_shapes=[pltpu.VMEM((tm, D), BF16), pltpu.VMEM((tm, D), F32)],
        compiler_params=_cparams(("arbitrary", "arbitrary")),
        name="mlp_ln",
    )(x, w1, w2, g, b)


def _attn_kernel(bb, q_ref, k_ref, v_ref, o_ref):
    scale = X_HEAD_DIM ** -0.5
    for b in range(bb):
        q = q_ref[b].astype(BF16)
        k = k_ref[b].astype(BF16)
        v = v_ref[b].astype(BF16)
        s = lax.dot_general(q, k, (((1,), (1,)), ((), ())), preferred_element_type=F32) * scale
        m = jnp.max(s, axis=-1, keepdims=True)
        e = jnp.exp(s - m)
        p = e / jnp.sum(e, axis=-1, keepdims=True)
        o_ref[b] = jnp.dot(p.astype(BF16), v, preferred_element_type=F32)


def _softmax_pv(s, v):
    m = jnp.max(s, axis=-1, keepdims=True)
    e = jnp.exp(s - m)
    p = e / jnp.sum(e, axis=-1, keepdims=True)
    return jnp.dot(p.astype(BF16), v, preferred_element_type=F32)


def _attn_cache_kernel(bb, q_ref, k_ref, v_ref, o_ref):
    scale = X_HEAD_DIM ** -0.5
    T = q_ref.shape[1]
    rows = N_MEM * X_HEADS
    q_head = lax.broadcasted_iota(jnp.int32, (X_HEADS * T, rows), 0) // T
    k_head = lax.broadcasted_iota(jnp.int32, (X_HEADS * T, rows), 1) & (X_HEADS - 1)
    own_head = q_head == k_head
    for b in range(bb):
        q = jnp.concatenate([q_ref[b, :, h * X_HEAD_DIM:(h + 1) * X_HEAD_DIM] for h in range(X_HEADS)], axis=0)
        k = k_ref[b].reshape(rows, X_HEAD_DIM).astype(BF16)
        v = v_ref[b].reshape(rows, X_HEAD_DIM).astype(BF16)
        s = lax.dot_general(q.astype(BF16), k, (((1,), (1,)), ((), ())), preferred_element_type=F32) * scale
        o = _softmax_pv(jnp.where(own_head, s, -jnp.inf), v)
        for h in range(X_HEADS):
            o_ref[b, :, h * X_HEAD_DIM:(h + 1) * X_HEAD_DIM] = o[h * T:(h + 1) * T]


def _attention_cache(q, cache_k, cache_v, bb):
    B, T, D = q.shape
    kv_spec = lambda: pl.BlockSpec((bb, N_MEM, X_HEADS, X_HEAD_DIM), lambda b: (b, 0, 0, 0))
    return pl.pallas_call(
        functools.partial(_attn_cache_kernel, bb),
        grid=(B // bb,),
        in_specs=[pl.BlockSpec((bb, T, D), lambda b: (b, 0, 0)), kv_spec(), kv_spec()],
        out_specs=pl.BlockSpec((bb, T, D), lambda b: (b, 0, 0)),
        out_shape=jax.ShapeDtypeStruct((B, T, D), F32),
        compiler_params=_cparams(("arbitrary",)),
        name="cache_attention",
    )(q, cache_k, cache_v)


def _attention(q, mem_k, mem_v, bb, tq):
    B, T, D = q.shape
    return pl.pallas_call(
        functools.partial(_attn_kernel, bb),
        grid=(B // bb, X_HEADS, T // tq),
        in_specs=[pl.BlockSpec((bb, tq, X_HEAD_DIM), lambda b, h, t: (b, t, h)),
                  pl.BlockSpec((bb, N_MEM, X_HEAD_DIM), lambda b, h, t: (b, 0, h)),
                  pl.BlockSpec((bb, N_MEM, X_HEAD_DIM), lambda b, h, t: (b, 0, h))],
        out_specs=pl.BlockSpec((bb, tq, X_HEAD_DIM), lambda b, h, t: (b, t, h)),
        out_shape=jax.ShapeDtypeStruct((B, T, D), F32),
        compiler_params=_cparams(("arbitrary", "arbitrary", "arbitrary")),
        name="mem_attention",
    )(q, mem_k, mem_v)


def _softplus(x):
    return jnp.maximum(x, 0.0) + jnp.log1p(jnp.exp(-jnp.abs(x)))


def _lru_kernel(bb, tc, ux_ref, ug_ref, h0_ref, c0_ref, cw_ref, cb_ref, wa_ref, ba_ref, wx_ref, bx_ref,
                lam_ref, y_ref, hl_ref, cn_ref, ext_ref, hc_ref, a_ref, b_ref, hs_ref):
    ti = pl.program_id(1)
    pad = SUBLANES
    neg_c_sp = -LRU_C * _softplus(-lam_ref[...])

    for b in range(bb):
        @pl.when(ti == 0)
        def _():
            ext_ref[b, pad - 3:pad, :] = c0_ref[b]
            hc_ref[b] = h0_ref[b]

        ext_ref[b, pad:pad + tc, :] = ux_ref[b]
        xc = cb_ref[...] + cw_ref[0:1, :] * ext_ref[b, pad - 3:pad - 3 + tc, :]
        for j in range(1, CONV_WIDTH):
            xc = xc + cw_ref[j:j + 1, :] * ext_ref[b, pad - 3 + j:pad - 3 + j + tc, :]

        ga, gx = [], []
        for n in range(LRU_BLOCKS):
            xb = xc[:, n * LRU_BLOCK:(n + 1) * LRU_BLOCK].astype(BF16)
            ga.append(jnp.dot(xb, wa_ref[n], preferred_element_type=F32))
            gx.append(jnp.dot(xb, wx_ref[n], preferred_element_type=F32))
        r = jax.nn.sigmoid(jnp.concatenate(ga, axis=1) + ba_ref[...])
        i = jax.nn.sigmoid(jnp.concatenate(gx, axis=1) + bx_ref[...])
        log_a = neg_c_sp * r
        a_ref[...] = jnp.exp(log_a)
        b_ref[...] = jnp.sqrt(1.0 - jnp.exp(2.0 * log_a)) * (i * xc)

        def step(t, h):
            h = a_ref[pl.ds(t, 1), :] * h + b_ref[pl.ds(t, 1), :]
            hs_ref[pl.ds(t, 1), :] = h
            return h

        h = lax.fori_loop(0, tc, step, hc_ref[b])
        hc_ref[b] = h
        hl_ref[b] = h
        y_ref[b] = hs_ref[...] * jax.nn.gelu(ug_ref[b])
        tail = ext_ref[b, pad + tc - 3:pad + tc, :]
        cn_ref[b] = tail
        ext_ref[b, pad - 3:pad, :] = tail


def _lru(p3, h0, conv0, cw, cb, wa, ba, wx, bx, lam, bb, tc):
    B, T, _ = p3.shape
    W = LRU_WIDTH
    row = lambda: pl.BlockSpec((1, W), lambda b, t: (0, 0))
    blk = lambda: pl.BlockSpec((LRU_BLOCKS, LRU_BLOCK, LRU_BLOCK), lambda b, t: (0, 0, 0))
    return pl.pallas_call(
        functools.partial(_lru_kernel, bb, tc),
        grid=(B // bb, T // tc),
        in_specs=[pl.BlockSpec((bb, tc, W), lambda b, t: (b, t, 0)),
                  pl.BlockSpec((bb, tc, W), lambda b, t: (b, t, 1)),
                  pl.BlockSpec((bb, 1, W), lambda b, t: (b, 0, 0)),
                  pl.BlockSpec((bb, CONV_WIDTH - 1, W), lambda b, t: (b, 0, 0)),
                  pl.BlockSpec((CONV_WIDTH, W), lambda b, t: (0, 0)),
                  row(), blk(), row(), blk(), row(), row()],
        out_specs=[pl.BlockSpec((bb, tc, W), lambda b, t: (b, t, 0)),
                   pl.BlockSpec((bb, 1, W), lambda b, t: (b, 0, 0)),
                   pl.BlockSpec((bb, CONV_WIDTH - 1, W), lambda b, t: (b, 0, 0))],
        out_shape=[jax.ShapeDtypeStruct((B, T, W), F32),
                   jax.ShapeDtypeStruct((B, 1, W), F32),
                   jax.ShapeDtypeStruct((B, CONV_WIDTH - 1, W), F32)],
        scratch_shapes=[pltpu.VMEM((bb, tc + SUBLANES, W), F32),
                        pltpu.VMEM((bb, 1, W), F32),
                        pltpu.VMEM((tc, W), F32),
                        pltpu.VMEM((tc, W), F32),
                        pltpu.VMEM((tc, W), F32)],
        compiler_params=_cparams(("arbitrary", "arbitrary")),
        name="rg_lru",
    )(p3, p3, h0, conv0, cw, cb, wa, ba, wx, bx, lam)


def _head_ones(n):
    ri = lax.broadcasted_iota(jnp.int32, (n, n), 0)
    ci = lax.broadcasted_iota(jnp.int32, (n, n), 1)
    return ((ri >> 6) == (ci >> 6)).astype(BF16)


def _head_sum(x, ones):
    outs = []
    for c in range(HEAD_PAIRS):
        xc = x[:, c * LANES:(c + 1) * LANES]
        hi = xc.astype(BF16)
        r1 = xc - hi.astype(F32)
        mid = r1.astype(BF16)
        lo = (r1 - mid.astype(F32)).astype(BF16)
        s = jnp.dot(hi, ones, preferred_element_type=F32)
        s = s + jnp.dot(mid, ones, preferred_element_type=F32)
        s = s + jnp.dot(lo, ones, preferred_element_type=F32)
        outs.append(s)
    return jnp.concatenate(outs, axis=1)


def _rwkv_kernel(bb, tc, ur_ref, uk_ref, uv_ref, ul_ref, sh0_ref, s0_ref, mu_ref, w0_ref, ww_ref, a0_ref,
                 wa_ref, wg_ref, kk_ref, ka_ref, rk_ref, lnw_ref, lnb_ref,
                 y_ref, st_ref,
                 s_ref, carry_ref, kap_ref, kapa_ref, w_ref, km_ref, v_ref, r_ref, ob_ref):
    ti = pl.program_id(1)
    ones = _head_ones(LANES)
    ones2 = _head_ones(2 * LANES)
    W = RWKV_WIDTH
    NH = RWKV_HEAD
    row0 = lax.broadcasted_iota(jnp.int32, (tc, 1), 0) == 0
    sub_i = lax.broadcasted_iota(jnp.int32, (NH, LANES), 0)
    lane_i = lax.broadcasted_iota(jnp.int32, (NH, LANES), 1)
    lane_t = lane_i & (NH - 1)
    eye16 = (lane_t == sub_i).astype(BF16)
    gb = 2 if bb % 2 == 0 else 1

    @pl.when(ti == 0)
    def _():
        for b in range(bb):
            for c in range(HEAD_PAIRS):
                s_ref[b, c, :, 0:NH] = s0_ref[b, 2 * c]
                s_ref[b, c, :, NH:2 * NH] = s0_ref[b, 2 * c + 1]
        carry_ref[...] = sh0_ref[...]
        ob_ref[...] = jnp.zeros_like(ob_ref)

    def shifted(u, b, lo, width):
        prev = jnp.where(row0, carry_ref[b, :, lo:lo + width], pltpu.roll(u, shift=1, axis=0))
        return u + (prev - u) * mu_ref[:, lo:lo + width]

    zs = [[], [], [], []]
    for b in range(bb):
        ur, uk, uv, ul = ur_ref[b], uk_ref[b], uv_ref[b], ul_ref[b]
        zs[0].append(shifted(ur, b, 0, W))
        zs[1].append(shifted(uk, b, W, W))
        zs[2].append(shifted(uv, b, 2 * W, W))
        zs[3].append(shifted(ul, b, 3 * W, LORA_PAD))
        carry_ref[b, :, 0:W] = ur[tc - 1:tc, :]
        carry_ref[b, :, W:2 * W] = uk[tc - 1:tc, :]
        carry_ref[b, :, 2 * W:3 * W] = uv[tc - 1:tc, :]
        carry_ref[b, :, 3 * W:3 * W + LORA_PAD] = ul[tc - 1:tc, :]
    zr, zk, zv, zl = (jnp.concatenate(z, axis=0) for z in zs)

    z_wa = zl[:, 0:LORA_WA_PAD]
    z_g = zl[:, LORA_WA_PAD:LORA_WA_PAD + LORA_G_PAD]
    dw = jnp.dot(jnp.tanh(z_wa).astype(BF16), ww_ref[...], preferred_element_type=F32)
    da = jnp.dot(z_wa.astype(BF16), wa_ref[...], preferred_element_type=F32)
    gate = jnp.dot(jax.nn.sigmoid(z_g).astype(BF16), wg_ref[...], preferred_element_type=F32)
    a = jax.nn.sigmoid(a0_ref[...] + da)
    kk = zk * kk_ref[...]
    kap = kk * lax.rsqrt(jnp.maximum(_head_sum(kk * kk, ones), 1e-24))
    kap_ref[...] = kap
    kapa_ref[...] = kap * a
    w_ref[...] = jnp.exp(-DECAY_SCALE * jax.nn.sigmoid(w0_ref[...] + dw))
    km_ref[...] = zk * (1.0 + (a - 1.0) * ka_ref[...])
    v_ref[...] = zv
    r_ref[...] = zr

    def lanes(c):
        return slice(c * LANES, (c + 1) * LANES)

    def out_products(t_out, with_update, t):
        omask = lane_t == t_out
        for g0 in range(0, bb, gb):
            group = range(g0, g0 + gb)
            lhs, rows = [], []
            for b in group:
                r_p = r_ref[pl.ds(b * tc + jnp.maximum(t_out, 0), 1), :]
                if with_update:
                    row = pl.ds(b * tc + t, 1)
                    kap_t, v_t = kap_ref[row, :], v_ref[row, :]
                    rows.append((kapa_ref[row, :], w_ref[row, :], km_ref[row, :]))
                    for c0 in range(0, HEAD_PAIRS, 2):
                        dv = []
                        for c in (c0, c0 + 1):
                            s = s_ref[b, c]
                            lhs.append(jnp.concatenate([(s * kap_t[:, lanes(c)]).astype(BF16),
                                                        (s * r_p[:, lanes(c)]).astype(BF16)], axis=1))
                            dv.append(eye16 * jnp.broadcast_to(v_t[:, lanes(c)], (NH, LANES)).astype(BF16))
                        lhs.append(jnp.concatenate(dv, axis=1))
                else:
                    for c in range(0, HEAD_PAIRS, 2):
                        lhs.append(jnp.concatenate([(s_ref[b, c] * r_p[:, lanes(c)]).astype(BF16),
                                                    (s_ref[b, c + 1] * r_p[:, lanes(c + 1)]).astype(BF16)], axis=1))
            red = jnp.dot(jnp.concatenate(lhs, axis=0), ones2, preferred_element_type=F32)
            for bi, b in enumerate(group):
                for c in range(HEAD_PAIRS):
                    if with_update:
                        base = ((bi * (HEAD_PAIRS // 2) + c // 2) * 3 + c % 2) * NH
                        skk = red[base:base + NH, 0:LANES]
                        ob = red[base:base + NH, LANES:2 * LANES]
                        vbase = ((bi * (HEAD_PAIRS // 2) + c // 2) * 3 + 2) * NH
                        vb = red[vbase:vbase + NH, (c % 2) * LANES:(c % 2 + 1) * LANES]
                        kapa_t, w_t, km_t = rows[bi]
                        s_ref[b, c] = (s_ref[b, c] * w_t[:, lanes(c)] - skk * kapa_t[:, lanes(c)]
                                       + vb * km_t[:, lanes(c)])
                    else:
                        base = (bi * (HEAD_PAIRS // 2) + c // 2) * NH
                        ob = red[base:base + NH, (c % 2) * LANES:(c % 2 + 1) * LANES]
                    pltpu.store(ob_ref.at[b, c], ob, mask=omask)

    def step(t, carry):
        out_products(t - 1, True, t)
        return carry

    lax.fori_loop(0, tc, step, 0, unroll=2)
    out_products(tc - 1, False, None)
    for b in range(bb):
        for c in range(HEAD_PAIRS):
            st_ref[b, 2 * c] = s_ref[b, c, :, 0:NH]
            st_ref[b, 2 * c + 1] = s_ref[b, c, :, NH:2 * NH]

    inv_n = 1.0 / RWKV_HEAD
    left = lane_i < NH
    o_rows = []
    for b in range(bb):
        o_chunks = []
        for c in range(0, HEAD_PAIRS, 2):
            xt = jnp.concatenate([ob_ref[b, c], ob_ref[b, c + 1]], axis=0).T
            top, bot = xt[0:NH], xt[NH:2 * NH]
            o_chunks.append(jnp.where(left, top, pltpu.roll(bot, shift=NH, axis=1))[0:tc])
            o_chunks.append(jnp.where(left, pltpu.roll(top, shift=NH, axis=1), bot)[0:tc])
        o_rows.append(jnp.concatenate(o_chunks, axis=1))
    o = jnp.concatenate(o_rows, axis=0)
    mean = _head_sum(o, ones) * inv_n
    d = o - mean
    var = _head_sum(d * d, ones) * inv_n
    on = d * lax.rsqrt(var + GN_EPS) * lnw_ref[...] + lnb_ref[...]
    bonus = _head_sum(r_ref[...] * km_ref[...] * rk_ref[...], ones) * v_ref[...]
    y = (on + bonus) * gate
    for b in range(bb):
        y_ref[b] = y[b * tc:(b + 1) * tc]


def _rwkv(p3, shift0, s0, mu, w0, ww, a0, wa, wg, kk, ka, rk, lnw, lnb, bb, tc):
    B, T, _ = p3.shape
    W = RWKV_WIDTH
    SW = 3 * W + LORA_PAD
    row = lambda: pl.BlockSpec((1, W), lambda b, t: (0, 0))
    lora = lambda rows: pl.BlockSpec((rows, W), lambda b, t: (0, 0))
    state = lambda: pl.BlockSpec((bb, RWKV_HEADS, RWKV_HEAD, RWKV_HEAD), lambda b, t: (b, 0, 0, 0))
    chunk = lambda: pltpu.VMEM((bb * tc, W), F32)
    return pl.pallas_call(
        functools.partial(_rwkv_kernel, bb, tc),
        grid=(B // bb, T // tc),
        in_specs=[pl.BlockSpec((bb, tc, W), lambda b, t: (b, t, 2)),
                  pl.BlockSpec((bb, tc, W), lambda b, t: (b, t, 3)),
                  pl.BlockSpec((bb, tc, W), lambda b, t: (b, t, 4)),
                  pl.BlockSpec((bb, tc, LORA_PAD), lambda b, t: (b, t, LORA_COL_BLOCK)),
                  pl.BlockSpec((bb, 1, SW), lambda b, t: (b, 0, 0)),
                  state(),
                  pl.BlockSpec((1, SW), lambda b, t: (0, 0)),
                  row(), lora(LORA_WA_PAD), row(), lora(LORA_WA_PAD), lora(LORA_G_PAD),
                  row(), row(), row(), row(), row()],
        out_specs=[pl.BlockSpec((bb, tc, W), lambda b, t: (b, t, 0)), state()],
        out_shape=[jax.ShapeDtypeStruct((B, T, W), F32),
                   jax.ShapeDtypeStruct((B, RWKV_HEADS, RWKV_HEAD, RWKV_HEAD), F32)],
        scratch_shapes=[pltpu.VMEM((bb, HEAD_PAIRS, RWKV_HEAD, LANES), F32),
                        pltpu.VMEM((bb, 1, SW), F32),
                        chunk(), chunk(), chunk(), chunk(), chunk(), chunk(),
                        pltpu.VMEM((bb, HEAD_PAIRS, RWKV_HEAD, LANES), F32)],
        compiler_params=_cparams(("arbitrary", "arbitrary")),
        name="rwkv7",
    )(p3, p3, p3, p3, shift0, s0, mu, w0, ww, a0, wa, wg, kk, ka, rk, lnw, lnb)


def _pad_shift(a):
    pad = [(0, 0)] * (a.ndim - 1) + [(0, LORA_PAD - LORA_TOTAL)]
    return jnp.pad(a, pad)


def _lora_rows(w_up, offset, rows):
    rank = w_up.shape[0]
    return jnp.pad(w_up, ((offset, rows - offset - rank), (0, 0))).astype(BF16)


def _hybrid_layer(x, mem_k, mem_v, h0, conv0, S0, shift0, wts, cfg):
    B, T, D = x.shape
    M = B * T
    x2 = x.reshape(M, D)

    p = _matmul(x2, wts["w_in"], cfg["tm_mm"], LORA_PAD, w_tail=wts["w_in_tail"])
    p3 = p.reshape(B, T, P_PAD)

    y_lru, h_last, conv_new = _lru(p3, h0.reshape(B, 1, LRU_WIDTH), conv0,
                                   wts["conv_w"], wts["conv_b"], wts["lru_wa"], wts["lru_ba"],
                                   wts["lru_wx"], wts["lru_bx"], wts["lru_L"], cfg["lru_bb"], cfg["lru_tc"])

    y_rwkv, s_new = _rwkv(p3, _pad_shift(shift0).reshape(B, 1, -1), S0,
                          wts["mu"], wts["w0"], wts["ww"], wts["a0"], wts["wa"], wts["wg"],
                          wts["k_k"], wts["k_a"], wts["r_k"], wts["ln_w"], wts["ln_b"],
                          cfg["rwkv_bb"], cfg["rwkv_tc"])
    shift_new = p3[:, T - 1, 2 * LRU_WIDTH:P_TOTAL]

    x1 = _matmul_res_ln([y_lru.reshape(M, LRU_WIDTH), y_rwkv.reshape(M, RWKV_WIDTH)],
                        [wts["w_out_a"], wts["w_out_b"]], x2, wts["ln1_g"], wts["ln1_b"], cfg["tm_ln"])

    q = _matmul(x1, wts["wq"], cfg["tm_mm"], 512)
    if mem_k.ndim == 4:
        att = _attention_cache(q.reshape(B, T, D), mem_k, mem_v, cfg["att_bb"])
    else:
        att = _attention(q.reshape(B, T, D), mem_k, mem_v, cfg["att_bb"], cfg["att_tq"])
    x2n = _matmul_res_ln([att.reshape(M, D)], [wts["wo"]], x1, wts["ln2_g"], wts["ln2_b"], cfg["tm_ln"])

    x3 = _mlp_ln(x2n, wts["w1"], wts["w2"], wts["ln3_g"], wts["ln3_b"], cfg["tm"], cfg["tf"])
    return (x3.reshape(B, T, D), h_last.reshape(B, LRU_WIDTH), conv_new,
            s_new, shift_new)


def kernel(x_prompt, x_sample, mem_prompt, cache_mem_k, cache_mem_v, state_lru_h, state_lru_conv, state_rwkv_S, state_rwkv_shift, w_in, lru_conv_w, lru_conv_b, lru_wa, lru_ba, lru_wx, lru_bx, lru_L, rwkv_mu, rwkv_w0, rwkv_w_up, rwkv_a0, rwkv_a_up, rwkv_g_up, rwkv_k_k, rwkv_k_a, rwkv_r_k, rwkv_ln_w, rwkv_ln_b, w_out, ln1_g, ln1_b, xa_wq, xa_wk, xa_wv, xa_wo, ln2_g, ln2_b, mlp_w1, mlp_w2, ln3_g, ln3_b):
    B, T, D = x_prompt.shape
    Bs, Ts, _ = x_sample.shape
    assert w_in.shape[0] == DEPTH

    cfg_p = dict(tm_mm=1024, tm=512, tf=1024, tm_ln=256, lru_bb=1, lru_tc=256, rwkv_bb=B, rwkv_tc=64,
                 att_bb=1, att_tq=512)
    cfg_s = dict(tm_mm=1024, tm=512, tf=1024, tm_ln=256, lru_bb=8, lru_tc=Ts, rwkv_bb=4, rwkv_tc=Ts,
                 att_bb=2, att_tq=Ts)

    yp, ys = x_prompt, x_sample
    outs_p = [[] for _ in range(6)]
    outs_s = [[] for _ in range(4)]
    for l in range(DEPTH):
        row = lambda a: a[l].reshape(1, -1)
        wts = dict(
            w_in=w_in[l],
            w_in_tail=jnp.pad(w_in[l][:, P_PAD - LORA_PAD:], ((0, 0), (0, P_PAD - P_TOTAL))).astype(BF16),
            conv_w=lru_conv_w[l], conv_b=row(lru_conv_b),
            lru_wa=lru_wa[l].astype(BF16), lru_ba=row(lru_ba),
            lru_wx=lru_wx[l].astype(BF16), lru_bx=row(lru_bx), lru_L=row(lru_L),
            mu=_pad_shift(rwkv_mu[l]).reshape(1, -1), w0=row(rwkv_w0), a0=row(rwkv_a0),
            ww=_lora_rows(rwkv_w_up[l], 0, LORA_WA_PAD), wa=_lora_rows(rwkv_a_up[l], DECAY_LORA, LORA_WA_PAD),
            wg=_lora_rows(rwkv_g_up[l], 0, LORA_G_PAD),
            k_k=row(rwkv_k_k), k_a=row(rwkv_k_a), r_k=row(rwkv_r_k), ln_w=row(rwkv_ln_w), ln_b=row(rwkv_ln_b),
            w_out_a=w_out[l, :LRU_WIDTH].astype(BF16), w_out_b=w_out[l, LRU_WIDTH:].astype(BF16),
            ln1_g=row(ln1_g), ln1_b=row(ln1_b),
            wq=xa_wq[l], wo=xa_wo[l].astype(BF16), ln2_g=row(ln2_g), ln2_b=row(ln2_b),
            w1=mlp_w1[l].astype(BF16), w2=mlp_w2[l].astype(BF16), ln3_g=row(ln3_g), ln3_b=row(ln3_b),
        )
        mem2 = mem_prompt.reshape(B * N_MEM, D)
        mem_k = _matmul(mem2, xa_wk[l], 512, 512).reshape(B, N_MEM, D)
        mem_v = _matmul(mem2, xa_wv[l], 512, 512).reshape(B, N_MEM, D)
        yp, hl, cl, Sl, shl = _hybrid_layer(
            yp, mem_k, mem_v,
            jnp.zeros((B, LRU_WIDTH), F32), jnp.zeros((B, CONV_WIDTH - 1, LRU_WIDTH), F32),
            jnp.zeros((B, RWKV_HEADS, RWKV_HEAD, RWKV_HEAD), F32), jnp.zeros((B, SHIFT_WIDTH), F32),
            wts, cfg_p)
        for lst, val in zip(outs_p, (mem_k.reshape(B, N_MEM, X_HEADS, X_HEAD_DIM),
                                     mem_v.reshape(B, N_MEM, X_HEADS, X_HEAD_DIM), hl, cl, Sl, shl)):
            lst.append(val)
        ys, hl, cl, Sl, shl = _hybrid_layer(
            ys, cache_mem_k[l], cache_mem_v[l],
            state_lru_h[l], state_lru_conv[l], state_rwkv_S[l], state_rwkv_shift[l], wts, cfg_s)
        for lst, val in zip(outs_s, (hl, cl, Sl, shl)):
            lst.append(val)

    return (yp, ys, *(jnp.stack(o) for o in outs_p), *(jnp.stack(o) for o in outs_s))
```

```python
import functools
import math

import jax
import jax.numpy as jnp
from jax import lax
from jax.experimental import pallas as pl
from jax.experimental.pallas import tpu as pltpu

F32 = jnp.float32
BF16 = jnp.bfloat16

D_MODEL = 2048
LRU_WIDTH = 1024
LRU_BLOCKS = 8
LRU_BLOCK = 128
CONV_WIDTH = 4
LRU_C = 8.0
RWKV_WIDTH = 1024
RWKV_HEAD = 64
RWKV_HEADS = 16
DECAY_LORA = 64
A_LORA = 64
G_LORA = 160
LORA_TOTAL = DECAY_LORA + A_LORA + G_LORA
SHIFT_WIDTH = 3 * RWKV_WIDTH + LORA_TOTAL
P_TOTAL = 2 * LRU_WIDTH + SHIFT_WIDTH
DECAY_SCALE = math.exp(-0.5)
N_MEM = 256
X_HEADS = 4
X_HEAD_DIM = 512
D_FF = 4 * D_MODEL
LN_EPS = 1e-5
GN_EPS = 64e-5
DEPTH = 1
ALPHA = (2 * DEPTH) ** 0.25

LANES = 128
SUBLANES = 8
VMEM_LIMIT_BYTES = 56 * 1024 * 1024

LORA_PAD = 512
LORA_WA_PAD = DECAY_LORA + A_LORA
LORA_G_PAD = 2 * LANES
P_PAD = 5 * 1024 + LORA_PAD
LORA_COL_BLOCK = (5 * 1024) // LORA_PAD
HEAD_PAIRS = RWKV_WIDTH // LANES


def _cparams(sem):
    return pltpu.CompilerParams(dimension_semantics=sem, vmem_limit_bytes=VMEM_LIMIT_BYTES)


def _mm_kernel(n_main, has_tail, w_is_nk, x_ref, w_ref, *rest):
    if has_tail:
        wt_ref, o_ref, xb_ref = rest
    else:
        o_ref, xb_ref = rest
    j = pl.program_id(1)
    contract = (((1,), (1,)), ((), ())) if w_is_nk else (((1,), (0,)), ((), ()))

    @pl.when(j == 0)
    def _():
        xb_ref[...] = x_ref[...].astype(BF16)

    @pl.when(j < n_main)
    def _():
        o_ref[...] = lax.dot_general(xb_ref[...], w_ref[...].astype(BF16), contract, preferred_element_type=F32)

    if has_tail:
        @pl.when(j == n_main)
        def _():
            o_ref[...] = lax.dot_general(xb_ref[...], wt_ref[...], contract, preferred_element_type=F32)


def _matmul(x, w, layer, tm, tn, w_tail=None, w_is_nk=False):
    M, K = x.shape
    N = w.shape[1] if w_is_nk else w.shape[2]
    n_main = N // tn
    has_tail = w_tail is not None
    n_tiles = n_main + int(has_tail)
    assert M % tm == 0 and (has_tail or N % tn == 0)
    if w_is_nk:
        w_spec = pl.BlockSpec((None, tn, K), lambda i, j: (layer, jnp.minimum(j, n_main - 1), 0))
    else:
        w_spec = pl.BlockSpec((None, K, tn), lambda i, j: (layer, 0, jnp.minimum(j, n_main - 1)))
    in_specs = [pl.BlockSpec((tm, K), lambda i, j: (i, 0)), w_spec]
    args = [x, w]
    if has_tail:
        in_specs.append(pl.BlockSpec(w_tail.shape, lambda i, j: (0, 0)))
        args.append(w_tail)
    return pl.pallas_call(
        functools.partial(_mm_kernel, n_main, has_tail, w_is_nk),
        grid=(M // tm, n_tiles),
        in_specs=in_specs,
        out_specs=pl.BlockSpec((tm, tn), lambda i, j: (i, j)),
        out_shape=jax.ShapeDtypeStruct((M, n_tiles * tn), F32),
        scratch_shapes=[pltpu.VMEM((tm, K), BF16)],
        compiler_params=_cparams(("arbitrary", "arbitrary")),
        name="matmul",
    )(*args)


def _layer_norm(y, g, b):
    mu = jnp.mean(y, axis=-1, keepdims=True)
    d = y - mu
    var = jnp.mean(d * d, axis=-1, keepdims=True)
    return d * lax.rsqrt(var + LN_EPS) * g + b


def _mm_ln_kernel(n_in, *refs):
    x_refs = refs[:n_in]
    w_refs = refs[n_in:2 * n_in]
    res_ref, g_ref, b_ref, o_ref = refs[2 * n_in:]
    acc = jnp.dot(x_refs[0][...].astype(BF16), w_refs[0][...], preferred_element_type=F32)
    for x_ref, w_ref in zip(x_refs[1:], w_refs[1:]):
        acc = acc + jnp.dot(x_ref[...].astype(BF16), w_ref[...], preferred_element_type=F32)
    y = ALPHA * res_ref[...] + acc
    o_ref[...] = _layer_norm(y, g_ref[...], b_ref[...])


def _matmul_res_ln(xs, ws, res, g, b, tm):
    M, N = res.shape
    n_in = len(xs)
    in_specs = [pl.BlockSpec((tm, x.shape[1]), lambda i: (i, 0)) for x in xs]
    in_specs += [pl.BlockSpec(w.shape, lambda i: (0, 0)) for w in ws]
    in_specs += [pl.BlockSpec((tm, N), lambda i: (i, 0)),
                 pl.BlockSpec((1, N), lambda i: (0, 0)),
                 pl.BlockSpec((1, N), lambda i: (0, 0))]
    return pl.pallas_call(
        functools.partial(_mm_ln_kernel, n_in),
        grid=(M // tm,),
        in_specs=in_specs,
        out_specs=pl.BlockSpec((tm, N), lambda i: (i, 0)),
        out_shape=jax.ShapeDtypeStruct((M, N), F32),
        compiler_params=_cparams(("arbitrary",)),
        name="matmul_res_ln",
    )(*xs, *ws, res, g, b)


def _mlp_kernel(x_ref, w1_ref, w2_ref, g_ref, b_ref, o_ref, xb_ref, acc_ref):
    f = pl.program_id(1)

    @pl.when(f == 0)
    def _():
        xb_ref[...] = x_ref[...].astype(BF16)
        acc_ref[...] = jnp.zeros_like(acc_ref)

    h = jnp.dot(xb_ref[...], w1_ref[...], preferred_element_type=F32)
    h = jnp.square(jnp.maximum(h, 0.0))
    acc_ref[...] += jnp.dot(h.astype(BF16), w2_ref[...], preferred_element_type=F32)

    @pl.when(f == pl.num_programs(1) - 1)
    def _():
        y = ALPHA * x_ref[...] + acc_ref[...]
        o_ref[...] = _layer_norm(y, g_ref[...], b_ref[...])


def _mlp_ln(x, w1, w2, g, b, tm, tf):
    M, D = x.shape
    FF = w1.shape[1]
    return pl.pallas_call(
        _mlp_kernel,
        grid=(M // tm, FF // tf),
        in_specs=[pl.BlockSpec((tm, D), lambda i, f: (i, 0)),
                  pl.BlockSpec((D, tf), lambda i, f: (0, f)),
                  pl.BlockSpec((tf, D), lambda i, f: (f, 0)),
                  pl.BlockSpec((1, D), lambda i, f: (0, 0)),
                  pl.BlockSpec((1, D), lambda i, f: (0, 0))],
        out_specs=pl.BlockSpec((tm, D), lambda i, f: (i, 0)),
        out_shape=jax.ShapeDtypeStruct((M, D), F32),
        scratch_shapes=[pltpu.VMEM((tm, D), BF16), pltpu.VMEM((tm, D), F32)],
        compiler_params=_cparams(("arbitrary", "arbitrary")),
        name="mlp_ln",
    )(x, w1, w2, g, b)


def _softmax_pv(s, v):
    m = jnp.max(s, axis=-1, keepdims=True)
    e = jnp.exp(s - m)
    p = e / jnp.sum(e, axis=-1, keepdims=True)
    return jnp.dot(p.astype(BF16), v, preferred_element_type=F32)


def _attn_cache_kernel(bb, q_ref, k_ref, v_ref, o_ref):
    scale = X_HEAD_DIM ** -0.5
    T = q_ref.shape[1]
    rows = N_MEM * X_HEADS
    q_head = lax.broadcasted_iota(jnp.int32, (X_HEADS * T, rows), 0) // T
    k_head = lax.broadcasted_iota(jnp.int32, (X_HEADS * T, rows), 1) & (X_HEADS - 1)
    own_head = q_head == k_head
    for b in range(bb):
        q = jnp.concatenate([q_ref[b, :, h * X_HEAD_DIM:(h + 1) * X_HEAD_DIM] for h in range(X_HEADS)], axis=0)
        k = k_ref[b].reshape(rows, X_HEAD_DIM).astype(BF16)
        v = v_ref[b].reshape(rows, X_HEAD_DIM).astype(BF16)
        s = lax.dot_general(q.astype(BF16), k, (((1,), (1,)), ((), ())), preferred_element_type=F32) * scale
        o = _softmax_pv(jnp.where(own_head, s, -jnp.inf), v)
        for h in range(X_HEADS):
            o_ref[b, :, h * X_HEAD_DIM:(h + 1) * X_HEAD_DIM] = o[h * T:(h + 1) * T]


def _attention_cache(q, cache_k, cache_v, bb):
    B, T, D = q.shape
    kv_spec = lambda: pl.BlockSpec((bb, N_MEM, X_HEADS, X_HEAD_DIM), lambda b: (b, 0, 0, 0))
    return pl.pallas_call(
        functools.partial(_attn_cache_kernel, bb),
        grid=(B // bb,),
        in_specs=[pl.BlockSpec((bb, T, D), lambda b: (b, 0, 0)), kv_spec(), kv_spec()],
        out_specs=pl.BlockSpec((bb, T, D), lambda b: (b, 0, 0)),
        out_shape=jax.ShapeDtypeStruct((B, T, D), F32),
        compiler_params=_cparams(("arbitrary",)),
        name="cache_attention",
    )(q, cache_k, cache_v)


def _xattn_block_kernel(x_ref, k_ref, v_ref, wq_ref, wo_ref, g_ref, b_ref, o_ref, kb_ref, vb_ref):
    @pl.when(pl.program_id(1) == 0)
    def _():
        kb_ref[...] = k_ref[0].astype(BF16)
        vb_ref[...] = v_ref[0].astype(BF16)

    scale = X_HEAD_DIM ** -0.5
    x = x_ref[0]
    q = jnp.dot(x.astype(BF16), wq_ref[...], preferred_element_type=F32)
    heads = []
    for h in range(X_HEADS):
        cols = slice(h * X_HEAD_DIM, (h + 1) * X_HEAD_DIM)
        s = lax.dot_general(q[:, cols].astype(BF16), kb_ref[:, cols], (((1,), (1,)), ((), ())),
                            preferred_element_type=F32) * scale
        heads.append(_softmax_pv(s, vb_ref[:, cols]).astype(BF16))
    att = jnp.concatenate(heads, axis=1)
    y = ALPHA * x + jnp.dot(att, wo_ref[...], preferred_element_type=F32)
    o_ref[0] = _layer_norm(y, g_ref[...], b_ref[...])


def _xattn_block(x, mem_k, mem_v, wq, wo, g, b, tq):
    B, T, D = x.shape
    resident = lambda: pl.BlockSpec((D, D), lambda i, t: (0, 0), pipeline_mode=pl.Buffered(1))
    kv = lambda: pl.BlockSpec((1, N_MEM, D), lambda i, t: (i, 0, 0))
    row = lambda: pl.BlockSpec((1, D), lambda i, t: (0, 0))
    return pl.pallas_call(
        _xattn_block_kernel,
        grid=(B, T // tq),
        in_specs=[pl.BlockSpec((1, tq, D), lambda i, t: (i, t, 0)), kv(), kv(), resident(), resident(), row(), row()],
        out_specs=pl.BlockSpec((1, tq, D), lambda i, t: (i, t, 0)),
        out_shape=jax.ShapeDtypeStruct((B, T, D), F32),
        scratch_shapes=[pltpu.VMEM((N_MEM, D), BF16), pltpu.VMEM((N_MEM, D), BF16)],
        compiler_params=_cparams(("arbitrary", "arbitrary")),
        name="xattn_block",
    )(x, mem_k, mem_v, wq, wo, g, b)


def _softplus(x):
    return jnp.maximum(x, 0.0) + jnp.log1p(jnp.exp(-jnp.abs(x)))


def _lru_kernel(bb, tc, ux_ref, ug_ref, h0_ref, c0_ref, cw_ref, cb_ref, wa_ref, ba_ref, wx_ref, bx_ref,
                lam_ref, y_ref, hl_ref, cn_ref, ext_ref, hc_ref, a_ref, b_ref, hs_ref):
    ti = pl.program_id(1)
    pad = SUBLANES
    neg_c_sp = -LRU_C * _softplus(-lam_ref[...])

    for b in range(bb):
        @pl.when(ti == 0)
        def _():
            ext_ref[b, pad - 3:pad, :] = c0_ref[b]
            hc_ref[b] = h0_ref[b]

        ext_ref[b, pad:pad + tc, :] = ux_ref[b]
        xc = cb_ref[...] + cw_ref[0:1, :] * ext_ref[b, pad - 3:pad - 3 + tc, :]
        for j in range(1, CONV_WIDTH):
            xc = xc + cw_ref[j:j + 1, :] * ext_ref[b, pad - 3 + j:pad - 3 + j + tc, :]

        ga, gx = [], []
        for n in range(LRU_BLOCKS):
            xb = xc[:, n * LRU_BLOCK:(n + 1) * LRU_BLOCK].astype(BF16)
            ga.append(jnp.dot(xb, wa_ref[n], preferred_element_type=F32))
            gx.append(jnp.dot(xb, wx_ref[n], preferred_element_type=F32))
        r = jax.nn.sigmoid(jnp.concatenate(ga, axis=1) + ba_ref[...])
        i = jax.nn.sigmoid(jnp.concatenate(gx, axis=1) + bx_ref[...])
        log_a = neg_c_sp * r
        rs = slice(b * tc, (b + 1) * tc)
        a_ref[rs, :] = jnp.exp(log_a)
        b_ref[rs, :] = jnp.sqrt(1.0 - jnp.exp(2.0 * log_a)) * (i * xc)
        tail = ext_ref[b, pad + tc - 3:pad + tc, :]
        cn_ref[b] = tail
        ext_ref[b, pad - 3:pad, :] = tail

    def step(t, hs):
        out = []
        for b in range(bb):
            row = pl.ds(b * tc + t, 1)
            h = a_ref[row, :] * hs[b] + b_ref[row, :]
            hs_ref[row, :] = h
            out.append(h)
        return tuple(out)

    hs = lax.fori_loop(0, tc, step, tuple(hc_ref[b] for b in range(bb)))
    for b in range(bb):
        hc_ref[b] = hs[b]
        hl_ref[b] = hs[b]
        y_ref[b] = hs_ref[b * tc:(b + 1) * tc, :] * jax.nn.gelu(ug_ref[b])


def _lru(p3, h0, conv0, cw, cb, wa, ba, wx, bx, lam, bb, tc):
    B, T, _ = p3.shape
    W = LRU_WIDTH
    row = lambda: pl.BlockSpec((1, W), lambda b, t: (0, 0))
    blk = lambda: pl.BlockSpec((LRU_BLOCKS, LRU_BLOCK, LRU_BLOCK), lambda b, t: (0, 0, 0))
    return pl.pallas_call(
        functools.partial(_lru_kernel, bb, tc),
        grid=(B // bb, T // tc),
        in_specs=[pl.BlockSpec((bb, tc, W), lambda b, t: (b, t, 0)),
                  pl.BlockSpec((bb, tc, W), lambda b, t: (b, t, 1)),
                  pl.BlockSpec((bb, 1, W), lambda b, t: (b, 0, 0)),
                  pl.BlockSpec((bb, CONV_WIDTH - 1, W), lambda b, t: (b, 0, 0)),
                  pl.BlockSpec((CONV_WIDTH, W), lambda b, t: (0, 0)),
                  row(), blk(), row(), blk(), row(), row()],
        out_specs=[pl.BlockSpec((bb, tc, W), lambda b, t: (b, t, 0)),
                   pl.BlockSpec((bb, 1, W), lambda b, t: (b, 0, 0)),
                   pl.BlockSpec((bb, CONV_WIDTH - 1, W), lambda b, t: (b, 0, 0))],
        out_shape=[jax.ShapeDtypeStruct((B, T, W), F32),
                   jax.ShapeDtypeStruct((B, 1, W), F32),
                   jax.ShapeDtypeStruct((B, CONV_WIDTH - 1, W), F32)],
        scratch_shapes=[pltpu.VMEM((bb, tc + SUBLANES, W), F32),
                        pltpu.VMEM((bb, 1, W), F32),
                        pltpu.VMEM((bb * tc, W), F32),
                        pltpu.VMEM((bb * tc, W), F32),
                        pltpu.VMEM((bb * tc, W), F32)],
        compiler_params=_cparams(("arbitrary", "arbitrary")),
        name="rg_lru",
    )(p3, p3, h0, conv0, cw, cb, wa, ba, wx, bx, lam)


def _head_ones(n):
    ri = lax.broadcasted_iota(jnp.int32, (n, n), 0)
    ci = lax.broadcasted_iota(jnp.int32, (n, n), 1)
    return ((ri >> 6) == (ci >> 6)).astype(BF16)


def _head_sum(x, ones):
    outs = []
    for c in range(HEAD_PAIRS):
        xc = x[:, c * LANES:(c + 1) * LANES]
        hi = xc.astype(BF16)
        r1 = xc - hi.astype(F32)
        mid = r1.astype(BF16)
        lo = (r1 - mid.astype(F32)).astype(BF16)
        s = jnp.dot(hi, ones, preferred_element_type=F32)
        s = s + jnp.dot(mid, ones, preferred_element_type=F32)
        s = s + jnp.dot(lo, ones, preferred_element_type=F32)
        outs.append(s)
    return jnp.concatenate(outs, axis=1)


def _rwkv_kernel(bb, tc, ur_ref, uk_ref, uv_ref, ul_ref, sh0_ref, s0_ref, mu_ref, w0_ref, ww_ref, a0_ref,
                 wa_ref, wg_ref, kk_ref, ka_ref, rk_ref, lnw_ref, lnb_ref,
                 y_ref, st_ref,
                 s_ref, carry_ref, kap_ref, kapa_ref, w_ref, km_ref, v_ref, r_ref, ob_ref):
    ti = pl.program_id(1)
    ones = _head_ones(LANES)
    ones2 = _head_ones(2 * LANES)
    W = RWKV_WIDTH
    NH = RWKV_HEAD
    row0 = lax.broadcasted_iota(jnp.int32, (tc, 1), 0) == 0
    sub_i = lax.broadcasted_iota(jnp.int32, (NH, LANES), 0)
    lane_i = lax.broadcasted_iota(jnp.int32, (NH, LANES), 1)
    lane_t = lane_i & (NH - 1)
    eye16 = (lane_t == sub_i).astype(BF16)
    gb = 2 if bb % 2 == 0 else 1

    @pl.when(ti == 0)
    def _():
        for b in range(bb):
            for c in range(HEAD_PAIRS):
                s_ref[b, c, :, 0:NH] = s0_ref[b, 2 * c]
                s_ref[b, c, :, NH:2 * NH] = s0_ref[b, 2 * c + 1]
        carry_ref[...] = sh0_ref[...]
        ob_ref[...] = jnp.zeros_like(ob_ref)

    def shifted(u, b, lo, width):
        prev = jnp.where(row0, carry_ref[b, :, lo:lo + width], pltpu.roll(u, shift=1, axis=0))
        return u + (prev - u) * mu_ref[:, lo:lo + width]

    zs = [[], [], [], []]
    for b in range(bb):
        ur, uk, uv, ul = ur_ref[b], uk_ref[b], uv_ref[b], ul_ref[b]
        zs[0].append(shifted(ur, b, 0, W))
        zs[1].append(shifted(uk, b, W, W))
        zs[2].append(shifted(uv, b, 2 * W, W))
        zs[3].append(shifted(ul, b, 3 * W, LORA_PAD))
        carry_ref[b, :, 0:W] = ur[tc - 1:tc, :]
        carry_ref[b, :, W:2 * W] = uk[tc - 1:tc, :]
        carry_ref[b, :, 2 * W:3 * W] = uv[tc - 1:tc, :]
        carry_ref[b, :, 3 * W:3 * W + LORA_PAD] = ul[tc - 1:tc, :]
    zr, zk, zv, zl = (jnp.concatenate(z, axis=0) for z in zs)

    z_wa = zl[:, 0:LORA_WA_PAD]
    z_g = zl[:, LORA_WA_PAD:LORA_WA_PAD + LORA_G_PAD]
    dw = jnp.dot(jnp.tanh(z_wa).astype(BF16), ww_ref[...], preferred_element_type=F32)
    da = jnp.dot(z_wa.astype(BF16), wa_ref[...], preferred_element_type=F32)
    gate = jnp.dot(jax.nn.sigmoid(z_g).astype(BF16), wg_ref[...], preferred_element_type=F32)
    a = jax.nn.sigmoid(a0_ref[...] + da)
    kk = zk * kk_ref[...]
    kap = kk * lax.rsqrt(jnp.maximum(_head_sum(kk * kk, ones), 1e-24))
    kap_ref[...] = kap
    kapa_ref[...] = kap * a
    w_ref[...] = jnp.exp(-DECAY_SCALE * jax.nn.sigmoid(w0_ref[...] + dw))
    km_ref[...] = zk * (1.0 + (a - 1.0) * ka_ref[...])
    v_ref[...] = zv
    r_ref[...] = zr

    def lanes(c):
        return slice(c * LANES, (c + 1) * LANES)

    def out_products(t_out, with_update, t):
        omask = lane_t == t_out
        for g0 in range(0, bb, gb):
            group = range(g0, g0 + gb)
            lhs, rows = [], []
            for b in group:
                r_p = r_ref[pl.ds(b * tc + jnp.maximum(t_out, 0), 1), :]
                if with_update:
                    row = pl.ds(b * tc + t, 1)
                    kap_t, v_t = kap_ref[row, :], v_ref[row, :]
                    rows.append((kapa_ref[row, :], w_ref[row, :], km_ref[row, :]))
                    for c0 in range(0, HEAD_PAIRS, 2):
                        dv = []
                        for c in (c0, c0 + 1):
                            s = s_ref[b, c]
                            lhs.append(jnp.concatenate([(s * kap_t[:, lanes(c)]).astype(BF16),
                                                        (s * r_p[:, lanes(c)]).astype(BF16)], axis=1))
                            dv.append(eye16 * jnp.broadcast_to(v_t[:, lanes(c)], (NH, LANES)).astype(BF16))
                        lhs.append(jnp.concatenate(dv, axis=1))
                else:
                    for c in range(0, HEAD_PAIRS, 2):
                        lhs.append(jnp.concatenate([(s_ref[b, c] * r_p[:, lanes(c)]).astype(BF16),
                                                    (s_ref[b, c + 1] * r_p[:, lanes(c + 1)]).astype(BF16)], axis=1))
            red = jnp.dot(jnp.concatenate(lhs, axis=0), ones2, preferred_element_type=F32)
            for bi, b in enumerate(group):
                for c in range(HEAD_PAIRS):
                    if with_update:
                        base = ((bi * (HEAD_PAIRS // 2) + c // 2) * 3 + c % 2) * NH
                        skk = red[base:base + NH, 0:LANES]
                        ob = red[base:base + NH, LANES:2 * LANES]
                        vbase = ((bi * (HEAD_PAIRS // 2) + c // 2) * 3 + 2) * NH
                        vb = red[vbase:vbase + NH, (c % 2) * LANES:(c % 2 + 1) * LANES]
                        kapa_t, w_t, km_t = rows[bi]
                        s_ref[b, c] = (s_ref[b, c] * w_t[:, lanes(c)] - skk * kapa_t[:, lanes(c)]
                                       + vb * km_t[:, lanes(c)])
                    else:
                        base = (bi * (HEAD_PAIRS // 2) + c // 2) * NH
                        ob = red[base:base + NH, (c % 2) * LANES:(c % 2 + 1) * LANES]
                    pltpu.store(ob_ref.at[b, c], ob, mask=omask)

    def step(t, carry):
        out_products(t - 1, True, t)
        return carry

    lax.fori_loop(0, tc, step, 0, unroll=8)
    out_products(tc - 1, False, None)
    for b in range(bb):
        for c in range(HEAD_PAIRS):
            st_ref[b, 2 * c] = s_ref[b, c, :, 0:NH]
            st_ref[b, 2 * c + 1] = s_ref[b, c, :, NH:2 * NH]

    inv_n = 1.0 / RWKV_HEAD
    left = lane_i < NH
    o_rows = []
    for b in range(bb):
        o_chunks = []
        for c in range(0, HEAD_PAIRS, 2):
            xt = jnp.concatenate([ob_ref[b, c], ob_ref[b, c + 1]], axis=0).T
            top, bot = xt[0:NH], xt[NH:2 * NH]
            o_chunks.append(jnp.where(left, top, pltpu.roll(bot, shift=NH, axis=1))[0:tc])
            o_chunks.append(jnp.where(left, pltpu.roll(top, shift=NH, axis=1), bot)[0:tc])
        o_rows.append(jnp.concatenate(o_chunks, axis=1))
    o = jnp.concatenate(o_rows, axis=0)
    mean = _head_sum(o, ones) * inv_n
    d = o - mean
    var = _head_sum(d * d, ones) * inv_n
    on = d * lax.rsqrt(var + GN_EPS) * lnw_ref[...] + lnb_ref[...]
    bonus = _head_sum(r_ref[...] * km_ref[...] * rk_ref[...], ones) * v_ref[...]
    y = (on + bonus) * gate
    for b in range(bb):
        y_ref[b] = y[b * tc:(b + 1) * tc]


def _rwkv(p3, shift0, s0, layer, mu, w0, ww, a0, wa, wg, kk, ka, rk, lnw, lnb, bb, tc):
    B, T, _ = p3.shape
    W = RWKV_WIDTH
    SW = 3 * W + LORA_PAD
    row = lambda: pl.BlockSpec((1, W), lambda b, t: (0, 0))
    lora = lambda rows: pl.BlockSpec((rows, W), lambda b, t: (0, 0))
    state = lambda: pl.BlockSpec((bb, RWKV_HEADS, RWKV_HEAD, RWKV_HEAD), lambda b, t: (b, 0, 0, 0))
    state_in = pl.BlockSpec((None, bb, RWKV_HEADS, RWKV_HEAD, RWKV_HEAD), lambda b, t: (layer, b, 0, 0, 0))
    chunk = lambda: pltpu.VMEM((bb * tc, W), F32)
    return pl.pallas_call(
        functools.partial(_rwkv_kernel, bb, tc),
        grid=(B // bb, T // tc),
        in_specs=[pl.BlockSpec((bb, tc, W), lambda b, t: (b, t, 2)),
                  pl.BlockSpec((bb, tc, W), lambda b, t: (b, t, 3)),
                  pl.BlockSpec((bb, tc, W), lambda b, t: (b, t, 4)),
                  pl.BlockSpec((bb, tc, LORA_PAD), lambda b, t: (b, t, LORA_COL_BLOCK)),
                  pl.BlockSpec((bb, 1, SW), lambda b, t: (b, 0, 0)),
                  state_in,
                  pl.BlockSpec((1, SW), lambda b, t: (0, 0)),
                  row(), lora(LORA_WA_PAD), row(), lora(LORA_WA_PAD), lora(LORA_G_PAD),
                  row(), row(), row(), row(), row()],
        out_specs=[pl.BlockSpec((bb, tc, W), lambda b, t: (b, t, 0)), state()],
        out_shape=[jax.ShapeDtypeStruct((B, T, W), F32),
                   jax.ShapeDtypeStruct((B, RWKV_HEADS, RWKV_HEAD, RWKV_HEAD), F32)],
        scratch_shapes=[pltpu.VMEM((bb, HEAD_PAIRS, RWKV_HEAD, LANES), F32),
                        pltpu.VMEM((bb, 1, SW), F32),
                        chunk(), chunk(), chunk(), chunk(), chunk(), chunk(),
                        pltpu.VMEM((bb, HEAD_PAIRS, RWKV_HEAD, LANES), F32)],
        compiler_params=_cparams(("arbitrary", "arbitrary")),
        name="rwkv7",
    )(p3, p3, p3, p3, shift0, s0, mu, w0, ww, a0, wa, wg, kk, ka, rk, lnw, lnb)


def _pad_shift(a):
    pad = [(0, 0)] * (a.ndim - 1) + [(0, LORA_PAD - LORA_TOTAL)]
    return jnp.pad(a, pad)


def _lora_rows(w_up, offset, rows):
    rank = w_up.shape[0]
    return jnp.pad(w_up, ((offset, rows - offset - rank), (0, 0))).astype(BF16)


def _hybrid_layer(x, mem_k, mem_v, h0, conv0, S0, shift0, wts, layer, cfg):
    B, T, D = x.shape
    M = B * T
    x2 = x.reshape(M, D)

    p = _matmul(x2, wts["w_in_t"], layer, cfg["tm_mm"], LORA_PAD, w_tail=wts["w_in_tail"], w_is_nk=True)
    p3 = p.reshape(B, T, P_PAD)

    y_lru, h_last, conv_new = _lru(p3, h0.reshape(B, 1, LRU_WIDTH), conv0,
                                   wts["conv_w"], wts["conv_b"], wts["lru_wa"], wts["lru_ba"],
                                   wts["lru_wx"], wts["lru_bx"], wts["lru_L"], cfg["lru_bb"], cfg["lru_tc"])

    y_rwkv, s_new = _rwkv(p3, _pad_shift(shift0).reshape(B, 1, -1), S0, layer,
                          wts["mu"], wts["w0"], wts["ww"], wts["a0"], wts["wa"], wts["wg"],
                          wts["k_k"], wts["k_a"], wts["r_k"], wts["ln_w"], wts["ln_b"],
                          cfg["rwkv_bb"], cfg["rwkv_tc"])
    shift_new = p3[:, T - 1, 2 * LRU_WIDTH:P_TOTAL]

    x1 = _matmul_res_ln([y_lru.reshape(M, LRU_WIDTH), y_rwkv.reshape(M, RWKV_WIDTH)],
                        [wts["w_out_a"], wts["w_out_b"]], x2, wts["ln1_g"], wts["ln1_b"], cfg["tm_ln"])

    if mem_k.ndim == 4:
        q = _matmul(x1, wts["wq"], layer, cfg["tm_mm"], 512)
        att = _attention_cache(q.reshape(B, T, D), mem_k, mem_v, cfg["att_bb"])
        x2n = _matmul_res_ln([att.reshape(M, D)], [wts["wo"]], x1, wts["ln2_g"], wts["ln2_b"], cfg["tm_ln"])
    else:
        x2n = _xattn_block(x1.reshape(B, T, D), mem_k, mem_v, wts["wq_b"], wts["wo"],
                           wts["ln2_g"], wts["ln2_b"], cfg["att_tq"]).reshape(M, D)

    x3 = _mlp_ln(x2n, wts["w1"], wts["w2"], wts["ln3_g"], wts["ln3_b"], cfg["tm"], cfg["tf"])
    return (x3.reshape(B, T, D), h_last.reshape(B, LRU_WIDTH), conv_new,
            s_new, shift_new)


def kernel(x_prompt, x_sample, mem_prompt, cache_mem_k, cache_mem_v, state_lru_h, state_lru_conv, state_rwkv_S, state_rwkv_shift, w_in, lru_conv_w, lru_conv_b, lru_wa, lru_ba, lru_wx, lru_bx, lru_L, rwkv_mu, rwkv_w0, rwkv_w_up, rwkv_a0, rwkv_a_up, rwkv_g_up, rwkv_k_k, rwkv_k_a, rwkv_r_k, rwkv_ln_w, rwkv_ln_b, w_out, ln1_g, ln1_b, xa_wq, xa_wk, xa_wv, xa_wo, ln2_g, ln2_b, mlp_w1, mlp_w2, ln3_g, ln3_b):
    B, T, D = x_prompt.shape
    Bs, Ts, _ = x_sample.shape
    assert w_in.shape[0] == DEPTH

    cfg_p = dict(tm_mm=1024, tm=512, tf=1024, tm_ln=256, lru_bb=2, lru_tc=256, rwkv_bb=B, rwkv_tc=64,
                 att_bb=1, att_tq=256)
    cfg_s = dict(tm_mm=1024, tm=512, tf=1024, tm_ln=256, lru_bb=8, lru_tc=Ts, rwkv_bb=4, rwkv_tc=Ts,
                 att_bb=2, att_tq=Ts)

    yp, ys = x_prompt, x_sample
    outs_p = [[] for _ in range(6)]
    outs_s = [[] for _ in range(4)]
    for l in range(DEPTH):
        row = lambda a: a[l].reshape(1, -1)
        wts = dict(
            w_in_t=jnp.swapaxes(w_in, 1, 2),
            w_in_tail=jnp.pad(jnp.swapaxes(w_in, 1, 2)[l, P_PAD - LORA_PAD:],
                              ((0, P_PAD - P_TOTAL), (0, 0))).astype(BF16),
            conv_w=lru_conv_w[l], conv_b=row(lru_conv_b),
            lru_wa=lru_wa[l].astype(BF16), lru_ba=row(lru_ba),
            lru_wx=lru_wx[l].astype(BF16), lru_bx=row(lru_bx), lru_L=row(lru_L),
            mu=_pad_shift(rwkv_mu[l]).reshape(1, -1), w0=row(rwkv_w0), a0=row(rwkv_a0),
            ww=_lora_rows(rwkv_w_up[l], 0, LORA_WA_PAD), wa=_lora_rows(rwkv_a_up[l], DECAY_LORA, LORA_WA_PAD),
            wg=_lora_rows(rwkv_g_up[l], 0, LORA_G_PAD),
            k_k=row(rwkv_k_k), k_a=row(rwkv_k_a), r_k=row(rwkv_r_k), ln_w=row(rwkv_ln_w), ln_b=row(rwkv_ln_b),
            w_out_a=w_out[l, :LRU_WIDTH].astype(BF16), w_out_b=w_out[l, LRU_WIDTH:].astype(BF16),
            ln1_g=row(ln1_g), ln1_b=row(ln1_b),
            wq=xa_wq, wq_b=xa_wq[l].astype(BF16), wo=xa_wo[l].astype(BF16), ln2_g=row(ln2_g), ln2_b=row(ln2_b),
            w1=mlp_w1[l].astype(BF16), w2=mlp_w2[l].astype(BF16), ln3_g=row(ln3_g), ln3_b=row(ln3_b),
        )
        mem2 = mem_prompt.reshape(B * N_MEM, D)
        mem_k = _matmul(mem2, xa_wk, l, 512, 512).reshape(B, N_MEM, D)
        mem_v = _matmul(mem2, xa_wv, l, 512, 512).reshape(B, N_MEM, D)
        yp, hl, cl, Sl, shl = _hybrid_layer(
            yp, mem_k, mem_v,
            jnp.zeros((B, LRU_WIDTH), F32), jnp.zeros((B, CONV_WIDTH - 1, LRU_WIDTH), F32),
            jnp.zeros((1, B, RWKV_HEADS, RWKV_HEAD, RWKV_HEAD), F32), jnp.zeros((B, SHIFT_WIDTH), F32),
            wts, 0, cfg_p)
        for lst, val in zip(outs_p, (mem_k.reshape(B, N_MEM, X_HEADS, X_HEAD_DIM),
                                     mem_v.reshape(B, N_MEM, X_HEADS, X_HEAD_DIM), hl, cl, Sl, shl)):
            lst.append(val)
        ys, hl, cl, Sl, shl = _hybrid_layer(
            ys, cache_mem_k[l], cache_mem_v[l],
            state_lru_h[l], state_lru_conv[l], state_rwkv_S, state_rwkv_shift[l], wts, l, cfg_s)
        for lst, val in zip(outs_s, (hl, cl, Sl, shl)):
            lst.append(val)

    return (yp, ys, *(jnp.stack(o) for o in outs_p), *(jnp.stack(o) for o in outs_s))
```

```python
import functools
import math

import jax
import jax.numpy as jnp
from jax import lax
from jax.experimental import pallas as pl
from jax.experimental.pallas import tpu as pltpu

F32 = jnp.float32
BF16 = jnp.bfloat16

D_MODEL = 2048
LRU_WIDTH = 1024
LRU_BLOCKS = 8
LRU_BLOCK = 128
CONV_WIDTH = 4
LRU_C = 8.0
RWKV_WIDTH = 1024
RWKV_HEAD = 64
RWKV_HEADS = 16
DECAY_LORA = 64
A_LORA = 64
G_LORA = 160
LORA_TOTAL = DECAY_LORA + A_LORA + G_LORA
SHIFT_WIDTH = 3 * RWKV_WIDTH + LORA_TOTAL
P_TOTAL = 2 * LRU_WIDTH + SHIFT_WIDTH
DECAY_SCALE = math.exp(-0.5)
N_MEM = 256
X_HEADS = 4
X_HEAD_DIM = 512
D_FF = 4 * D_MODEL
LN_EPS = 1e-5
GN_EPS = 64e-5
DEPTH = 1
ALPHA = (2 * DEPTH) ** 0.25

LANES = 128
SUBLANES = 8
VMEM_LIMIT_BYTES = 56 * 1024 * 1024

LORA_PAD = 512
LORA_WA_PAD = DECAY_LORA + A_LORA
LORA_G_PAD = 2 * LANES
P_PAD = 5 * 1024 + LORA_PAD
LORA_COL_BLOCK = (5 * 1024) // LORA_PAD
HEAD_PAIRS = RWKV_WIDTH // LANES


def _cparams(sem):
    return pltpu.CompilerParams(dimension_semantics=sem, vmem_limit_bytes=VMEM_LIMIT_BYTES)


def _mm_kernel(n_main, has_tail, w_is_nk, x_ref, w_ref, *rest):
    if has_tail:
        wt_ref, o_ref, xb_ref = rest
    else:
        o_ref, xb_ref = rest
    j = pl.program_id(1)
    contract = (((1,), (1,)), ((), ())) if w_is_nk else (((1,), (0,)), ((), ()))

    @pl.when(j == 0)
    def _():
        xb_ref[...] = x_ref[...].astype(BF16)

    @pl.when(j < n_main)
    def _():
        o_ref[...] = lax.dot_general(xb_ref[...], w_ref[...].astype(BF16), contract, preferred_element_type=F32)

    if has_tail:
        @pl.when(j == n_main)
        def _():
            o_ref[...] = lax.dot_general(xb_ref[...], wt_ref[...], contract, preferred_element_type=F32)


def _matmul(x, w, layer, tm, tn, w_tail=None, w_is_nk=False):
    M, K = x.shape
    N = w.shape[1] if w_is_nk else w.shape[2]
    n_main = N // tn
    has_tail = w_tail is not None
    n_tiles = n_main + int(has_tail)
    assert M % tm == 0 and (has_tail or N % tn == 0)
    if w_is_nk:
        w_spec = pl.BlockSpec((None, tn, K), lambda i, j: (layer, jnp.minimum(j, n_main - 1), 0))
    else:
        w_spec = pl.BlockSpec((None, K, tn), lambda i, j: (layer, 0, jnp.minimum(j, n_main - 1)))
    in_specs = [pl.BlockSpec((tm, K), lambda i, j: (i, 0)), w_spec]
    args = [x, w]
    if has_tail:
        in_specs.append(pl.BlockSpec(w_tail.shape, lambda i, j: (0, 0)))
        args.append(w_tail)
    return pl.pallas_call(
        functools.partial(_mm_kernel, n_main, has_tail, w_is_nk),
        grid=(M // tm, n_tiles),
        in_specs=in_specs,
        out_specs=pl.BlockSpec((tm, tn), lambda i, j: (i, j)),
        out_shape=jax.ShapeDtypeStruct((M, n_tiles * tn), F32),
        scratch_shapes=[pltpu.VMEM((tm, K), BF16)],
        compiler_params=_cparams(("arbitrary", "arbitrary")),
        name="matmul",
    )(*args)


def _layer_norm(y, g, b):
    mu = jnp.mean(y, axis=-1, keepdims=True)
    d = y - mu
    var = jnp.mean(d * d, axis=-1, keepdims=True)
    return d * lax.rsqrt(var + LN_EPS) * g + b


def _mm_ln_kernel(n_in, *refs):
    x_refs = refs[:n_in]
    w_refs = refs[n_in:2 * n_in]
    res_ref, g_ref, b_ref, o_ref = refs[2 * n_in:]
    acc = jnp.dot(x_refs[0][...].astype(BF16), w_refs[0][...], preferred_element_type=F32)
    for x_ref, w_ref in zip(x_refs[1:], w_refs[1:]):
        acc = acc + jnp.dot(x_ref[...].astype(BF16), w_ref[...], preferred_element_type=F32)
    y = ALPHA * res_ref[...] + acc
    o_ref[...] = _layer_norm(y, g_ref[...], b_ref[...])


def _matmul_res_ln(xs, ws, res, g, b, tm):
    M, N = res.shape
    n_in = len(xs)
    in_specs = [pl.BlockSpec((tm, x.shape[1]), lambda i: (i, 0)) for x in xs]
    in_specs += [pl.BlockSpec(w.shape, lambda i: (0, 0)) for w in ws]
    in_specs += [pl.BlockSpec((tm, N), lambda i: (i, 0)),
                 pl.BlockSpec((1, N), lambda i: (0, 0)),
                 pl.BlockSpec((1, N), lambda i: (0, 0))]
    return pl.pallas_call(
        functools.partial(_mm_ln_kernel, n_in),
        grid=(M // tm,),
        in_specs=in_specs,
        out_specs=pl.BlockSpec((tm, N), lambda i: (i, 0)),
        out_shape=jax.ShapeDtypeStruct((M, N), F32),
        compiler_params=_cparams(("arbitrary",)),
        name="matmul_res_ln",
    )(*xs, *ws, res, g, b)


def _mlp_kernel(x_ref, w1_ref, w2_ref, g_ref, b_ref, o_ref, xb_ref, acc_ref):
    f = pl.program_id(1)

    @pl.when(f == 0)
    def _():
        xb_ref[...] = x_ref[...].astype(BF16)
        acc_ref[...] = jnp.zeros_like(acc_ref)

    h = jnp.dot(xb_ref[...], w1_ref[...], preferred_element_type=F32)
    h = jnp.square(jnp.maximum(h, 0.0))
    acc_ref[...] += jnp.dot(h.astype(BF16), w2_ref[...], preferred_element_type=F32)

    @pl.when(f == pl.num_programs(1) - 1)
    def _():
        y = ALPHA * x_ref[...] + acc_ref[...]
        o_ref[...] = _layer_norm(y, g_ref[...], b_ref[...])


def _mlp_ln(x, w1, w2, g, b, tm, tf):
    M, D = x.shape
    FF = w1.shape[1]
    return pl.pallas_call(
        _mlp_kernel,
        grid=(M // tm, FF // tf),
        in_specs=[pl.BlockSpec((tm, D), lambda i, f: (i, 0)),
                  pl.BlockSpec((D, tf), lambda i, f: (0, f)),
                  pl.BlockSpec((tf, D), lambda i, f: (f, 0)),
                  pl.BlockSpec((1, D), lambda i, f: (0, 0)),
                  pl.BlockSpec((1, D), lambda i, f: (0, 0))],
        out_specs=pl.BlockSpec((tm, D), lambda i, f: (i, 0)),
        out_shape=jax.ShapeDtypeStruct((M, D), F32),
        scratch_shapes=[pltpu.VMEM((tm, D), BF16), pltpu.VMEM((tm, D), F32)],
        compiler_params=_cparams(("arbitrary", "arbitrary")),
        name="mlp_ln",
    )(x, w1, w2, g, b)


def _softmax_pv(s, v):
    m = jnp.max(s, axis=-1, keepdims=True)
    e = jnp.exp(s - m)
    p = e / jnp.sum(e, axis=-1, keepdims=True)
    return jnp.dot(p.astype(BF16), v, preferred_element_type=F32)


def _attn_cache_kernel(bb, q_ref, k_ref, v_ref, o_ref):
    scale = X_HEAD_DIM ** -0.5
    T = q_ref.shape[1]
    rows = N_MEM * X_HEADS
    q_head = lax.broadcasted_iota(jnp.int32, (X_HEADS * T, rows), 0) // T
    k_head = lax.broadcasted_iota(jnp.int32, (X_HEADS * T, rows), 1) & (X_HEADS - 1)
    own_head = q_head == k_head
    for b in range(bb):
        q = jnp.concatenate([q_ref[b, :, h * X_HEAD_DIM:(h + 1) * X_HEAD_DIM] for h in range(X_HEADS)], axis=0)
        k = k_ref[b].reshape(rows, X_HEAD_DIM).astype(BF16)
        v = v_ref[b].reshape(rows, X_HEAD_DIM).astype(BF16)
        s = lax.dot_general(q.astype(BF16), k, (((1,), (1,)), ((), ())), preferred_element_type=F32) * scale
        o = _softmax_pv(jnp.where(own_head, s, -jnp.inf), v)
        for h in range(X_HEADS):
            o_ref[b, :, h * X_HEAD_DIM:(h + 1) * X_HEAD_DIM] = o[h * T:(h + 1) * T]


def _attention_cache(q, cache_k, cache_v, bb):
    B, T, D = q.shape
    kv_spec = lambda: pl.BlockSpec((bb, N_MEM, X_HEADS, X_HEAD_DIM), lambda b: (b, 0, 0, 0))
    return pl.pallas_call(
        functools.partial(_attn_cache_kernel, bb),
        grid=(B // bb,),
        in_specs=[pl.BlockSpec((bb, T, D), lambda b: (b, 0, 0)), kv_spec(), kv_spec()],
        out_specs=pl.BlockSpec((bb, T, D), lambda b: (b, 0, 0)),
        out_shape=jax.ShapeDtypeStruct((B, T, D), F32),
        compiler_params=_cparams(("arbitrary",)),
        name="cache_attention",
    )(q, cache_k, cache_v)


def _xattn_block_kernel(x_ref, k_ref, v_ref, wq_ref, wo_ref, g_ref, b_ref, o_ref, kb_ref, vb_ref):
    @pl.when(pl.program_id(1) == 0)
    def _():
        kb_ref[...] = k_ref[0].astype(BF16)
        vb_ref[...] = v_ref[0].astype(BF16)

    scale = X_HEAD_DIM ** -0.5
    x = x_ref[0]
    q = jnp.dot(x.astype(BF16), wq_ref[...], preferred_element_type=F32)
    heads = []
    for h in range(X_HEADS):
        cols = slice(h * X_HEAD_DIM, (h + 1) * X_HEAD_DIM)
        s = lax.dot_general(q[:, cols].astype(BF16), kb_ref[:, cols], (((1,), (1,)), ((), ())),
                            preferred_element_type=F32) * scale
        heads.append(_softmax_pv(s, vb_ref[:, cols]).astype(BF16))
    att = jnp.concatenate(heads, axis=1)
    y = ALPHA * x + jnp.dot(att, wo_ref[...], preferred_element_type=F32)
    o_ref[0] = _layer_norm(y, g_ref[...], b_ref[...])


def _xattn_block(x, mem_k, mem_v, wq, wo, g, b, tq):
    B, T, D = x.shape
    resident = lambda: pl.BlockSpec((D, D), lambda i, t: (0, 0), pipeline_mode=pl.Buffered(1))
    kv = lambda: pl.BlockSpec((1, N_MEM, D), lambda i, t: (i, 0, 0))
    row = lambda: pl.BlockSpec((1, D), lambda i, t: (0, 0))
    return pl.pallas_call(
        _xattn_block_kernel,
        grid=(B, T // tq),
        in_specs=[pl.BlockSpec((1, tq, D), lambda i, t: (i, t, 0)), kv(), kv(), resident(), resident(), row(), row()],
        out_specs=pl.BlockSpec((1, tq, D), lambda i, t: (i, t, 0)),
        out_shape=jax.ShapeDtypeStruct((B, T, D), F32),
        scratch_shapes=[pltpu.VMEM((N_MEM, D), BF16), pltpu.VMEM((N_MEM, D), BF16)],
        compiler_params=_cparams(("arbitrary", "arbitrary")),
        name="xattn_block",
    )(x, mem_k, mem_v, wq, wo, g, b)


def _softplus(x):
    return jnp.maximum(x, 0.0) + jnp.log1p(jnp.exp(-jnp.abs(x)))


def _lru_kernel(bb, tc, ux_ref, ug_ref, h0_ref, c0_ref, cw_ref, cb_ref, wa_ref, ba_ref, wx_ref, bx_ref,
                lam_ref, y_ref, hl_ref, cn_ref, ext_ref, hc_ref, a_ref, b_ref, hs_ref):
    ti = pl.program_id(1)
    pad = SUBLANES
    neg_c_sp = -LRU_C * _softplus(-lam_ref[...])

    for b in range(bb):
        @pl.when(ti == 0)
        def _():
            ext_ref[b, pad - 3:pad, :] = c0_ref[b]
            hc_ref[b] = h0_ref[b]

        ext_ref[b, pad:pad + tc, :] = ux_ref[b]
        xc = cb_ref[...] + cw_ref[0:1, :] * ext_ref[b, pad - 3:pad - 3 + tc, :]
        for j in range(1, CONV_WIDTH):
            xc = xc + cw_ref[j:j + 1, :] * ext_ref[b, pad - 3 + j:pad - 3 + j + tc, :]

        ga, gx = [], []
        for n in range(LRU_BLOCKS):
            xb = xc[:, n * LRU_BLOCK:(n + 1) * LRU_BLOCK].astype(BF16)
            ga.append(jnp.dot(xb, wa_ref[n], preferred_element_type=F32))
            gx.append(jnp.dot(xb, wx_ref[n], preferred_element_type=F32))
        r = jax.nn.sigmoid(jnp.concatenate(ga, axis=1) + ba_ref[...])
        i = jax.nn.sigmoid(jnp.concatenate(gx, axis=1) + bx_ref[...])
        log_a = neg_c_sp * r
        rs = slice(b * tc, (b + 1) * tc)
        a_ref[rs, :] = jnp.exp(log_a)
        b_ref[rs, :] = jnp.sqrt(1.0 - jnp.exp(2.0 * log_a)) * (i * xc)
        tail = ext_ref[b, pad + tc - 3:pad + tc, :]
        cn_ref[b] = tail
        ext_ref[b, pad - 3:pad, :] = tail

    def step(t, hs):
        out = []
        for b in range(bb):
            row = pl.ds(b * tc + t, 1)
            h = a_ref[row, :] * hs[b] + b_ref[row, :]
            hs_ref[row, :] = h
            out.append(h)
        return tuple(out)

    hs = lax.fori_loop(0, tc, step, tuple(hc_ref[b] for b in range(bb)))
    for b in range(bb):
        hc_ref[b] = hs[b]
        hl_ref[b] = hs[b]
        y_ref[b] = hs_ref[b * tc:(b + 1) * tc, :] * jax.nn.gelu(ug_ref[b])


def _lru(p3, h0, conv0, cw, cb, wa, ba, wx, bx, lam, bb, tc):
    B, T, _ = p3.shape
    W = LRU_WIDTH
    row = lambda: pl.BlockSpec((1, W), lambda b, t: (0, 0))
    blk = lambda: pl.BlockSpec((LRU_BLOCKS, LRU_BLOCK, LRU_BLOCK), lambda b, t: (0, 0, 0))
    return pl.pallas_call(
        functools.partial(_lru_kernel, bb, tc),
        grid=(B // bb, T // tc),
        in_specs=[pl.BlockSpec((bb, tc, W), lambda b, t: (b, t, 0)),
                  pl.BlockSpec((bb, tc, W), lambda b, t: (b, t, 1)),
                  pl.BlockSpec((bb, 1, W), lambda b, t: (b, 0, 0)),
                  pl.BlockSpec((bb, CONV_WIDTH - 1, W), lambda b, t: (b, 0, 0)),
                  pl.BlockSpec((CONV_WIDTH, W), lambda b, t: (0, 0)),
                  row(), blk(), row(), blk(), row(), row()],
        out_specs=[pl.BlockSpec((bb, tc, W), lambda b, t: (b, t, 0)),
                   pl.BlockSpec((bb, 1, W), lambda b, t: (b, 0, 0)),
                   pl.BlockSpec((bb, CONV_WIDTH - 1, W), lambda b, t: (b, 0, 0))],
        out_shape=[jax.ShapeDtypeStruct((B, T, W), F32),
                   jax.ShapeDtypeStruct((B, 1, W), F32),
                   jax.ShapeDtypeStruct((B, CONV_WIDTH - 1, W), F32)],
        scratch_shapes=[pltpu.VMEM((bb, tc + SUBLANES, W), F32),
                        pltpu.VMEM((bb, 1, W), F32),
                        pltpu.VMEM((bb * tc, W), F32),
                        pltpu.VMEM((bb * tc, W), F32),
                        pltpu.VMEM((bb * tc, W), F32)],
        compiler_params=_cparams(("arbitrary", "arbitrary")),
        name="rg_lru",
    )(p3, p3, h0, conv0, cw, cb, wa, ba, wx, bx, lam)


def _head_ones(n):
    ri = lax.broadcasted_iota(jnp.int32, (n, n), 0)
    ci = lax.broadcasted_iota(jnp.int32, (n, n), 1)
    return ((ri >> 6) == (ci >> 6)).astype(BF16)


def _split3(x):
    hi = x.astype(BF16)
    r1 = x - hi.astype(F32)
    mid = r1.astype(BF16)
    return hi, mid, (r1 - mid.astype(F32)).astype(BF16)


def _head_sum(x, ones):
    return sum(jnp.dot(t, ones, preferred_element_type=F32) for t in _split3(x))


def _dot_nt(a, b):
    return lax.dot_general(a, b, (((1,), (1,)), ((), ())), preferred_element_type=F32)


def _rwkv_kernel(bb, tc, ur_ref, uk_ref, uv_ref, ul_ref, sh0_ref, s0_ref, mu_ref, w0_ref, ww_ref, a0_ref,
                 wa_ref, wg_ref, kk_ref, ka_ref, rk_ref, lnw_ref, lnb_ref,
                 y_ref, st_ref,
                 s_ref, carry_ref, kap_ref, kapa_ref, w_ref, km_ref, v_ref, r_ref, gate_ref, ob_ref,
                 u_ref, kt_ref, bt_ref, rhm_ref, av_ref, au_ref):
    ti = pl.program_id(1)
    ones = _head_ones(LANES)
    ones2 = _head_ones(2 * LANES)
    W = RWKV_WIDTH
    NH = RWKV_HEAD
    bulk_o = tc == NH
    row0 = lax.broadcasted_iota(jnp.int32, (tc, 1), 0) == 0
    sub_i = lax.broadcasted_iota(jnp.int32, (NH, LANES), 0)
    lane_i = lax.broadcasted_iota(jnp.int32, (NH, LANES), 1)
    lane_t = lane_i & (NH - 1)
    left = lane_i < NH
    eye16 = (lane_t == sub_i).astype(BF16)
    gb = 2 if bb % 2 == 0 else 1

    def lanes(c):
        return slice(c * LANES, (c + 1) * LANES)

    def rows(b):
        return slice(b * tc, (b + 1) * tc)

    @pl.when(ti == 0)
    def _():
        for b in range(bb):
            for c in range(HEAD_PAIRS):
                s_ref[b, c, :, 0:NH] = s0_ref[b, 2 * c]
                s_ref[b, c, :, NH:2 * NH] = s0_ref[b, 2 * c + 1]
        carry_ref[...] = sh0_ref[...]
        ob_ref[...] = jnp.zeros_like(ob_ref)

    def shifted(u, b, lo):
        cols = slice(lo, lo + u.shape[1])
        prev = jnp.where(row0, carry_ref[b, :, cols], pltpu.roll(u, shift=1, axis=0))
        return u + (prev - u) * mu_ref[:, cols]

    def head_sums(x):
        return jnp.concatenate([_head_sum(x[:, lanes(c)], ones) for c in range(HEAD_PAIRS)], axis=1)

    zs = [[], [], [], []]
    for b in range(bb):
        for z, src, lo in zip(zs, (ur_ref, uk_ref, uv_ref, ul_ref), (0, W, 2 * W, 3 * W)):
            z.append(shifted(src[b], b, lo))
    for b in range(bb):
        for src, lo in zip((ur_ref, uk_ref, uv_ref, ul_ref), (0, W, 2 * W, 3 * W)):
            carry_ref[b, :, lo:lo + src.shape[2]] = src[b, tc - 1:tc, :]
    zr, zk, zv, zl = (jnp.concatenate(z, axis=0) for z in zs)

    z_wa = zl[:, 0:LORA_WA_PAD]
    z_g = zl[:, LORA_WA_PAD:LORA_WA_PAD + LORA_G_PAD]
    dw = jnp.dot(jnp.tanh(z_wa).astype(BF16), ww_ref[...], preferred_element_type=F32)
    da = jnp.dot(z_wa.astype(BF16), wa_ref[...], preferred_element_type=F32)
    gate_ref[...] = jnp.dot(jax.nn.sigmoid(z_g).astype(BF16), wg_ref[...], preferred_element_type=F32)
    log_w = -DECAY_SCALE * jax.nn.sigmoid(w0_ref[...] + dw)
    a = jax.nn.sigmoid(a0_ref[...] + da)
    kk = zk * kk_ref[...]
    kap = kk * lax.rsqrt(jnp.maximum(head_sums(kk * kk), 1e-24))
    km = zk * (1.0 + (a - 1.0) * ka_ref[...])
    kap_ref[...] = kap
    kapa_ref[...] = kap * a
    w_ref[...] = jnp.exp(log_w)
    km_ref[...] = km
    v_ref[...] = zv
    r_ref[...] = zr

    def head_rows(x):
        return jnp.concatenate([jnp.where(left, x, 0.0), jnp.where(left, 0.0, x)], axis=0).astype(BF16)

    if bulk_o:
        tri = (lax.broadcasted_iota(jnp.int32, (tc, tc), 0)
               >= lax.broadcasted_iota(jnp.int32, (tc, tc), 1)).astype(BF16)
        cum = jnp.concatenate([sum(jnp.dot(tri, t, preferred_element_type=F32) for t in _split3(log_w[rows(b)]))
                               for b in range(bb)], axis=0)
        inv_gamma = jnp.exp(-cum)
        rt = zr * jnp.exp(cum)
        kt_ref[...] = km * inv_gamma
        bt_ref[...] = kap * a * inv_gamma
        for b in range(bb):
            for c in range(HEAD_PAIRS):
                rhm = head_rows(rt[rows(b), lanes(c)])
                rhm_ref[b, c] = rhm
                ob_ref[b, c] = _dot_nt(s_ref[b, c].astype(BF16), rhm)

    def out_products(t_out, with_update, t):
        omask = lane_t == t_out
        for g0 in range(0, bb, gb):
            group = range(g0, g0 + gb)
            lhs, rowvecs = [], []
            for b in group:
                if not bulk_o:
                    r_p = r_ref[pl.ds(b * tc + jnp.maximum(t_out, 0), 1), :]
                if with_update:
                    row = pl.ds(b * tc + t, 1)
                    kap_t, v_t = kap_ref[row, :], v_ref[row, :]
                    rowvecs.append((kapa_ref[row, :], w_ref[row, :], km_ref[row, :]))
                    for c0 in range(0, HEAD_PAIRS, 2):
                        dv, ps = [], []
                        for c in (c0, c0 + 1):
                            s = s_ref[b, c]
                            ps.append((s * kap_t[:, lanes(c)]).astype(BF16))
                            if not bulk_o:
                                lhs.append(jnp.concatenate([ps.pop(), (s * r_p[:, lanes(c)]).astype(BF16)], axis=1))
                            dv.append(eye16 * jnp.broadcast_to(v_t[:, lanes(c)], (NH, LANES)).astype(BF16))
                        if bulk_o:
                            lhs.append(jnp.concatenate(ps, axis=1))
                        lhs.append(jnp.concatenate(dv, axis=1))
                else:
                    for c in range(0, HEAD_PAIRS, 2):
                        lhs.append(jnp.concatenate([(s_ref[b, c] * r_p[:, lanes(c)]).astype(BF16),
                                                    (s_ref[b, c + 1] * r_p[:, lanes(c + 1)]).astype(BF16)], axis=1))
            red = jnp.dot(jnp.concatenate(lhs, axis=0), ones2, preferred_element_type=F32)
            for bi, b in enumerate(group):
                for c in range(HEAD_PAIRS):
                    half = slice((c % 2) * LANES, (c % 2 + 1) * LANES)
                    if with_update and bulk_o:
                        base = (bi * (HEAD_PAIRS // 2) + c // 2) * 2 * NH
                        skk = red[base:base + NH, half]
                        vb = red[base + NH:base + 2 * NH, half]
                    elif with_update:
                        base = ((bi * (HEAD_PAIRS // 2) + c // 2) * 3 + c % 2) * NH
                        skk = red[base:base + NH, 0:LANES]
                        ob = red[base:base + NH, LANES:2 * LANES]
                        vbase = ((bi * (HEAD_PAIRS // 2) + c // 2) * 3 + 2) * NH
                        vb = red[vbase:vbase + NH, half]
                    else:
                        base = (bi * (HEAD_PAIRS // 2) + c // 2) * NH
                        ob = red[base:base + NH, half]
                    if with_update:
                        kapa_t, w_t, km_t = rowvecs[bi]
                        s_ref[b, c] = (s_ref[b, c] * w_t[:, lanes(c)] - skk * kapa_t[:, lanes(c)]
                                       + vb * km_t[:, lanes(c)])
                    if bulk_o:
                        pltpu.store(u_ref.at[b, c], skk, mask=omask)
                    else:
                        pltpu.store(ob_ref.at[b, c], ob, mask=omask)

    if bulk_o:
        def step(t, carry):
            out_products(t, True, t)
            return carry

        lax.fori_loop(0, tc, step, 0, unroll=8)
    else:
        def step(t, carry):
            out_products(t - 1, True, t)
            return carry

        lax.fori_loop(0, tc, step, 0, unroll=8)
        out_products(tc - 1, False, None)

    for b in range(bb):
        for c in range(HEAD_PAIRS):
            st_ref[b, 2 * c] = s_ref[b, c, :, 0:NH]
            st_ref[b, 2 * c + 1] = s_ref[b, c, :, NH:2 * NH]

    if bulk_o:
        rr = lax.broadcasted_iota(jnp.int32, (LANES, LANES), 0) & (NH - 1)
        cc = lax.broadcasted_iota(jnp.int32, (LANES, LANES), 1) & (NH - 1)
        causal = rr <= cc
        pad_rows = jnp.zeros((LANES - tc, LANES), F32)
        for b in range(bb):
            for c in range(HEAD_PAIRS):
                rhm = rhm_ref[b, c]
                av_ref[b, c] = jnp.where(causal, _dot_nt(head_rows(kt_ref[rows(b), lanes(c)]), rhm), 0.0).astype(BF16)
                au_ref[b, c] = jnp.where(causal, _dot_nt(head_rows(bt_ref[rows(b), lanes(c)]), rhm), 0.0).astype(BF16)
        for b in range(bb):
            for c in range(HEAD_PAIRS):
                vt = jnp.concatenate([v_ref[rows(b), lanes(c)], pad_rows], axis=0).T
                v_cols = jnp.where(left, vt[0:NH], pltpu.roll(vt[NH:2 * NH], shift=NH, axis=1))
                ob_ref[b, c] = (ob_ref[b, c]
                                + jnp.dot(v_cols.astype(BF16), av_ref[b, c], preferred_element_type=F32)
                                - jnp.dot(u_ref[b, c].astype(BF16), au_ref[b, c], preferred_element_type=F32))

    inv_n = 1.0 / RWKV_HEAD
    o_rows = []
    for b in range(bb):
        o_chunks = []
        for c in range(0, HEAD_PAIRS, 2):
            xt = jnp.concatenate([ob_ref[b, c], ob_ref[b, c + 1]], axis=0).T
            top, bot = xt[0:NH], xt[NH:2 * NH]
            o_chunks.append(jnp.where(left, top, pltpu.roll(bot, shift=NH, axis=1))[0:tc])
            o_chunks.append(jnp.where(left, pltpu.roll(top, shift=NH, axis=1), bot)[0:tc])
        o_rows.append(jnp.concatenate(o_chunks, axis=1))
    o = jnp.concatenate(o_rows, axis=0)
    mean = head_sums(o) * inv_n
    d = o - mean
    var = head_sums(d * d) * inv_n
    on = d * lax.rsqrt(var + GN_EPS) * lnw_ref[...] + lnb_ref[...]
    bonus = head_sums(r_ref[...] * km_ref[...] * rk_ref[...]) * v_ref[...]
    y = (on + bonus) * gate_ref[...]
    for b in range(bb):
        y_ref[b] = y[rows(b)]


def _rwkv(p3, shift0, s0, layer, mu, w0, ww, a0, wa, wg, kk, ka, rk, lnw, lnb, bb, tc):
    B, T, _ = p3.shape
    W = RWKV_WIDTH
    SW = 3 * W + LORA_PAD
    row = lambda: pl.BlockSpec((1, W), lambda b, t: (0, 0))
    lora = lambda rows: pl.BlockSpec((rows, W), lambda b, t: (0, 0))
    state = lambda: pl.BlockSpec((bb, RWKV_HEADS, RWKV_HEAD, RWKV_HEAD), lambda b, t: (b, 0, 0, 0))
    state_in = pl.BlockSpec((None, bb, RWKV_HEADS, RWKV_HEAD, RWKV_HEAD), lambda b, t: (layer, b, 0, 0, 0))
    chunk = lambda: pltpu.VMEM((bb * tc, W), F32)
    packed = lambda: pltpu.VMEM((bb, HEAD_PAIRS, RWKV_HEAD, LANES), F32)
    pair_mat = lambda: pltpu.VMEM((bb, HEAD_PAIRS, LANES, LANES), BF16)
    return pl.pallas_call(
        functools.partial(_rwkv_kernel, bb, tc),
        grid=(B // bb, T // tc),
        in_specs=[pl.BlockSpec((bb, tc, W), lambda b, t: (b, t, 2)),
                  pl.BlockSpec((bb, tc, W), lambda b, t: (b, t, 3)),
                  pl.BlockSpec((bb, tc, W), lambda b, t: (b, t, 4)),
                  pl.BlockSpec((bb, tc, LORA_PAD), lambda b, t: (b, t, LORA_COL_BLOCK)),
                  pl.BlockSpec((bb, 1, SW), lambda b, t: (b, 0, 0)),
                  state_in,
                  pl.BlockSpec((1, SW), lambda b, t: (0, 0)),
                  row(), lora(LORA_WA_PAD), row(), lora(LORA_WA_PAD), lora(LORA_G_PAD),
                  row(), row(), row(), row(), row()],
        out_specs=[pl.BlockSpec((bb, tc, W), lambda b, t: (b, t, 0)), state()],
        out_shape=[jax.ShapeDtypeStruct((B, T, W), F32),
                   jax.ShapeDtypeStruct((B, RWKV_HEADS, RWKV_HEAD, RWKV_HEAD), F32)],
        scratch_shapes=[packed(), pltpu.VMEM((bb, 1, SW), F32),
                        chunk(), chunk(), chunk(), chunk(), chunk(), chunk(), chunk(),
                        packed(), packed(), chunk(), chunk(), pair_mat(), pair_mat(), pair_mat()],
        compiler_params=_cparams(("arbitrary", "arbitrary")),
        name="rwkv7",
    )(p3, p3, p3, p3, shift0, s0, mu, w0, ww, a0, wa, wg, kk, ka, rk, lnw, lnb)


def _pad_shift(a):
    pad = [(0, 0)] * (a.ndim - 1) + [(0, LORA_PAD - LORA_TOTAL)]
    return jnp.pad(a, pad)


def _lora_rows(w_up, offset, rows):
    rank = w_up.shape[0]
    return jnp.pad(w_up, ((offset, rows - offset - rank), (0, 0))).astype(BF16)


def _hybrid_layer(x, mem_k, mem_v, h0, conv0, S0, shift0, wts, layer, cfg):
    B, T, D = x.shape
    M = B * T
    x2 = x.reshape(M, D)

    p = _matmul(x2, wts["w_in_t"], layer, cfg["tm_mm"], LORA_PAD, w_tail=wts["w_in_tail"], w_is_nk=True)
    p3 = p.reshape(B, T, P_PAD)

    y_lru, h_last, conv_new = _lru(p3, h0.reshape(B, 1, LRU_WIDTH), conv0,
                                   wts["conv_w"], wts["conv_b"], wts["lru_wa"], wts["lru_ba"],
                                   wts["lru_wx"], wts["lru_bx"], wts["lru_L"], cfg["lru_bb"], cfg["lru_tc"])

    y_rwkv, s_new = _rwkv(p3, _pad_shift(shift0).reshape(B, 1, -1), S0, layer,
                          wts["mu"], wts["w0"], wts["ww"], wts["a0"], wts["wa"], wts["wg"],
                          wts["k_k"], wts["k_a"], wts["r_k"], wts["ln_w"], wts["ln_b"],
                          cfg["rwkv_bb"], cfg["rwkv_tc"])
    shift_new = p3[:, T - 1, 2 * LRU_WIDTH:P_TOTAL]

    x1 = _matmul_res_ln([y_lru.reshape(M, LRU_WIDTH), y_rwkv.reshape(M, RWKV_WIDTH)],
                        [wts["w_out_a"], wts["w_out_b"]], x2, wts["ln1_g"], wts["ln1_b"], cfg["tm_ln"])

    if mem_k.ndim == 4:
        q = _matmul(x1, wts["wq"], layer, cfg["tm_mm"], 512)
        att = _attention_cache(q.reshape(B, T, D), mem_k, mem_v, cfg["att_bb"])
        x2n = _matmul_res_ln([att.reshape(M, D)], [wts["wo"]], x1, wts["ln2_g"], wts["ln2_b"], cfg["tm_ln"])
    else:
        x2n = _xattn_block(x1.reshape(B, T, D), mem_k, mem_v, wts["wq_b"], wts["wo"],
                           wts["ln2_g"], wts["ln2_b"], cfg["att_tq"]).reshape(M, D)

    x3 = _mlp_ln(x2n, wts["w1"], wts["w2"], wts["ln3_g"], wts["ln3_b"], cfg["tm"], cfg["tf"])
    return (x3.reshape(B, T, D), h_last.reshape(B, LRU_WIDTH), conv_new,
            s_new, shift_new)


def kernel(x_prompt, x_sample, mem_prompt, cache_mem_k, cache_mem_v, state_lru_h, state_lru_conv, state_rwkv_S, state_rwkv_shift, w_in, lru_conv_w, lru_conv_b, lru_wa, lru_ba, lru_wx, lru_bx, lru_L, rwkv_mu, rwkv_w0, rwkv_w_up, rwkv_a0, rwkv_a_up, rwkv_g_up, rwkv_k_k, rwkv_k_a, rwkv_r_k, rwkv_ln_w, rwkv_ln_b, w_out, ln1_g, ln1_b, xa_wq, xa_wk, xa_wv, xa_wo, ln2_g, ln2_b, mlp_w1, mlp_w2, ln3_g, ln3_b):
    B, T, D = x_prompt.shape
    Bs, Ts, _ = x_sample.shape
    assert w_in.shape[0] == DEPTH

    cfg_p = dict(tm_mm=1024, tm=512, tf=1024, tm_ln=256, lru_bb=2, lru_tc=256, rwkv_bb=B, rwkv_tc=64,
                 att_bb=1, att_tq=256)
    cfg_s = dict(tm_mm=1024, tm=512, tf=1024, tm_ln=256, lru_bb=8, lru_tc=Ts, rwkv_bb=4, rwkv_tc=Ts,
                 att_bb=2, att_tq=Ts)

    yp, ys = x_prompt, x_sample
    outs_p = [[] for _ in range(6)]
    outs_s = [[] for _ in range(4)]
    for l in range(DEPTH):
        row = lambda a: a[l].reshape(1, -1)
        wts = dict(
            w_in_t=jnp.swapaxes(w_in, 1, 2),
            w_in_tail=jnp.pad(jnp.swapaxes(w_in, 1, 2)[l, P_PAD - LORA_PAD:],
                              ((0, P_PAD - P_TOTAL), (0, 0))).astype(BF16),
            conv_w=lru_conv_w[l], conv_b=row(lru_conv_b),
            lru_wa=lru_wa[l].astype(BF16), lru_ba=row(lru_ba),
            lru_wx=lru_wx[l].astype(BF16), lru_bx=row(lru_bx), lru_L=row(lru_L),
            mu=_pad_shift(rwkv_mu[l]).reshape(1, -1), w0=row(rwkv_w0), a0=row(rwkv_a0),
            ww=_lora_rows(rwkv_w_up[l], 0, LORA_WA_PAD), wa=_lora_rows(rwkv_a_up[l], DECAY_LORA, LORA_WA_PAD),
            wg=_lora_rows(rwkv_g_up[l], 0, LORA_G_PAD),
            k_k=row(rwkv_k_k), k_a=row(rwkv_k_a), r_k=row(rwkv_r_k), ln_w=row(rwkv_ln_w), ln_b=row(rwkv_ln_b),
            w_out_a=w_out[l, :LRU_WIDTH].astype(BF16), w_out_b=w_out[l, LRU_WIDTH:].astype(BF16),
            ln1_g=row(ln1_g), ln1_b=row(ln1_b),
            wq=xa_wq, wq_b=xa_wq[l].astype(BF16), wo=xa_wo[l].astype(BF16), ln2_g=row(ln2_g), ln2_b=row(ln2_b),
            w1=mlp_w1[l].astype(BF16), w2=mlp_w2[l].astype(BF16), ln3_g=row(ln3_g), ln3_b=row(ln3_b),
        )
        mem2 = mem_prompt.reshape(B * N_MEM, D)
        mem_k = _matmul(mem2, xa_wk, l, 512, 512).reshape(B, N_MEM, D)
        mem_v = _matmul(mem2, xa_wv, l, 512, 512).reshape(B, N_MEM, D)
        yp, hl, cl, Sl, shl = _hybrid_layer(
            yp, mem_k, mem_v,
            jnp.zeros((B, LRU_WIDTH), F32), jnp.zeros((B, CONV_WIDTH - 1, LRU_WIDTH), F32),
            jnp.zeros((1, B, RWKV_HEADS, RWKV_HEAD, RWKV_HEAD), F32), jnp.zeros((B, SHIFT_WIDTH), F32),
            wts, 0, cfg_p)
        for lst, val in zip(outs_p, (mem_k.reshape(B, N_MEM, X_HEADS, X_HEAD_DIM),
                                     mem_v.reshape(B, N_MEM, X_HEADS, X_HEAD_DIM), hl, cl, Sl, shl)):
            lst.append(val)
        ys, hl, cl, Sl, shl = _hybrid_layer(
            ys, cache_mem_k[l], cache_mem_v[l],
            state_lru_h[l], state_lru_conv[l], state_rwkv_S, state_rwkv_shift[l], wts, l, cfg_s)
        for lst, val in zip(outs_s, (hl, cl, Sl, shl)):
            lst.append(val)

    return (yp, ys, *(jnp.stack(o) for o in outs_p), *(jnp.stack(o) for o in outs_s))
```

```python
import functools
import math

import jax
import jax.numpy as jnp
from jax import lax
from jax.experimental import pallas as pl
from jax.experimental.pallas import tpu as pltpu

F32 = jnp.float32
BF16 = jnp.bfloat16

D_MODEL = 2048
LRU_WIDTH = 1024
LRU_BLOCKS = 8
LRU_BLOCK = 128
CONV_WIDTH = 4
LRU_C = 8.0
RWKV_WIDTH = 1024
RWKV_HEAD = 64
RWKV_HEADS = 16
DECAY_LORA = 64
A_LORA = 64
G_LORA = 160
LORA_TOTAL = DECAY_LORA + A_LORA + G_LORA
SHIFT_WIDTH = 3 * RWKV_WIDTH + LORA_TOTAL
P_TOTAL = 2 * LRU_WIDTH + SHIFT_WIDTH
DECAY_SCALE = math.exp(-0.5)
N_MEM = 256
X_HEADS = 4
X_HEAD_DIM = 512
D_FF = 4 * D_MODEL
LN_EPS = 1e-5
GN_EPS = 64e-5
DEPTH = 1
ALPHA = (2 * DEPTH) ** 0.25

LANES = 128
SUBLANES = 8
VMEM_LIMIT_BYTES = 56 * 1024 * 1024

LORA_PAD = 512
LORA_WA_PAD = DECAY_LORA + A_LORA
LORA_G_PAD = 2 * LANES
P_PAD = 5 * 1024 + LORA_PAD
LORA_COL_BLOCK = (5 * 1024) // LORA_PAD
HEAD_PAIRS = RWKV_WIDTH // LANES


def _cparams(sem):
    return pltpu.CompilerParams(dimension_semantics=sem, vmem_limit_bytes=VMEM_LIMIT_BYTES)


def _mm_kernel(n_main, has_tail, w_is_nk, x_ref, w_ref, *rest):
    if has_tail:
        wt_ref, o_ref, xb_ref = rest
    else:
        o_ref, xb_ref = rest
    j = pl.program_id(1)
    contract = (((1,), (1,)), ((), ())) if w_is_nk else (((1,), (0,)), ((), ()))

    @pl.when(j == 0)
    def _():
        xb_ref[...] = x_ref[...].astype(BF16)

    @pl.when(j < n_main)
    def _():
        o_ref[...] = lax.dot_general(xb_ref[...], w_ref[...].astype(BF16), contract, preferred_element_type=F32)

    if has_tail:
        @pl.when(j == n_main)
        def _():
            o_ref[...] = lax.dot_general(xb_ref[...], wt_ref[...], contract, preferred_element_type=F32)


def _matmul(x, w, layer, tm, tn, w_tail=None, w_is_nk=False):
    M, K = x.shape
    N = w.shape[1] if w_is_nk else w.shape[2]
    n_main = N // tn
    has_tail = w_tail is not None
    n_tiles = n_main + int(has_tail)
    assert M % tm == 0 and (has_tail or N % tn == 0)
    if w_is_nk:
        w_spec = pl.BlockSpec((None, tn, K), lambda i, j: (layer, jnp.minimum(j, n_main - 1), 0))
    else:
        w_spec = pl.BlockSpec((None, K, tn), lambda i, j: (layer, 0, jnp.minimum(j, n_main - 1)))
    in_specs = [pl.BlockSpec((tm, K), lambda i, j: (i, 0)), w_spec]
    args = [x, w]
    if has_tail:
        in_specs.append(pl.BlockSpec(w_tail.shape, lambda i, j: (0, 0)))
        args.append(w_tail)
    return pl.pallas_call(
        functools.partial(_mm_kernel, n_main, has_tail, w_is_nk),
        grid=(M // tm, n_tiles),
        in_specs=in_specs,
        out_specs=pl.BlockSpec((tm, tn), lambda i, j: (i, j)),
        out_shape=jax.ShapeDtypeStruct((M, n_tiles * tn), F32),
        scratch_shapes=[pltpu.VMEM((tm, K), BF16)],
        compiler_params=_cparams(("arbitrary", "arbitrary")),
        name="matmul",
    )(*args)


def _layer_norm(y, g, b):
    mu = jnp.mean(y, axis=-1, keepdims=True)
    d = y - mu
    var = jnp.mean(d * d, axis=-1, keepdims=True)
    return d * lax.rsqrt(var + LN_EPS) * g + b


def _mm_ln_kernel(n_in, *refs):
    x_refs = refs[:n_in]
    w_refs = refs[n_in:2 * n_in]
    res_ref, g_ref, b_ref, o_ref = refs[2 * n_in:]
    acc = jnp.dot(x_refs[0][...].astype(BF16), w_refs[0][...], preferred_element_type=F32)
    for x_ref, w_ref in zip(x_refs[1:], w_refs[1:]):
        acc = acc + jnp.dot(x_ref[...].astype(BF16), w_ref[...], preferred_element_type=F32)
    y = ALPHA * res_ref[...] + acc
    o_ref[...] = _layer_norm(y, g_ref[...], b_ref[...])


def _matmul_res_ln(xs, ws, res, g, b, tm):
    M, N = res.shape
    n_in = len(xs)
    in_specs = [pl.BlockSpec((tm, x.shape[1]), lambda i: (i, 0)) for x in xs]
    in_specs += [pl.BlockSpec(w.shape, lambda i: (0, 0)) for w in ws]
    in_specs += [pl.BlockSpec((tm, N), lambda i: (i, 0)),
                 pl.BlockSpec((1, N), lambda i: (0, 0)),
                 pl.BlockSpec((1, N), lambda i: (0, 0))]
    return pl.pallas_call(
        functools.partial(_mm_ln_kernel, n_in),
        grid=(M // tm,),
        in_specs=in_specs,
        out_specs=pl.BlockSpec((tm, N), lambda i: (i, 0)),
        out_shape=jax.ShapeDtypeStruct((M, N), F32),
        compiler_params=_cparams(("arbitrary",)),
        name="matmul_res_ln",
    )(*xs, *ws, res, g, b)


def _mlp_kernel(x_ref, w1_ref, w2_ref, g_ref, b_ref, o_ref, xb_ref, acc_ref):
    f = pl.program_id(1)

    @pl.when(f == 0)
    def _():
        xb_ref[...] = x_ref[...].astype(BF16)
        acc_ref[...] = jnp.zeros_like(acc_ref)

    h = jnp.dot(xb_ref[...], w1_ref[...], preferred_element_type=F32)
    h = jnp.square(jnp.maximum(h, 0.0))
    acc_ref[...] += jnp.dot(h.astype(BF16), w2_ref[...], preferred_element_type=F32)

    @pl.when(f == pl.num_programs(1) - 1)
    def _():
        y = ALPHA * x_ref[...] + acc_ref[...]
        o_ref[...] = _layer_norm(y, g_ref[...], b_ref[...])


def _mlp_ln(x, w1, w2, g, b, tm, tf):
    M, D = x.shape
    FF = w1.shape[1]
    return pl.pallas_call(
        _mlp_kernel,
        grid=(M // tm, FF // tf),
        in_specs=[pl.BlockSpec((tm, D), lambda i, f: (i, 0)),
                  pl.BlockSpec((D, tf), lambda i, f: (0, f)),
                  pl.BlockSpec((tf, D), lambda i, f: (f, 0)),
                  pl.BlockSpec((1, D), lambda i, f: (0, 0)),
                  pl.BlockSpec((1, D), lambda i, f: (0, 0))],
        out_specs=pl.BlockSpec((tm, D), lambda i, f: (i, 0)),
        out_shape=jax.ShapeDtypeStruct((M, D), F32),
        scratch_shapes=[pltpu.VMEM((tm, D), BF16), pltpu.VMEM((tm, D), F32)],
        compiler_params=_cparams(("arbitrary", "arbitrary")),
        name="mlp_ln",
    )(x, w1, w2, g, b)


def _softmax_pv(s, v):
    m = jnp.max(s, axis=-1, keepdims=True)
    e = jnp.exp(s - m)
    p = e / jnp.sum(e, axis=-1, keepdims=True)
    return jnp.dot(p.astype(BF16), v, preferred_element_type=F32)


def _attn_cache_kernel(bb, q_ref, k_ref, v_ref, o_ref):
    scale = X_HEAD_DIM ** -0.5
    T = q_ref.shape[1]
    rows = N_MEM * X_HEADS
    q_head = lax.broadcasted_iota(jnp.int32, (X_HEADS * T, rows), 0) // T
    k_head = lax.broadcasted_iota(jnp.int32, (X_HEADS * T, rows), 1) & (X_HEADS - 1)
    own_head = q_head == k_head
    for b in range(bb):
        q = jnp.concatenate([q_ref[b, :, h * X_HEAD_DIM:(h + 1) * X_HEAD_DIM] for h in range(X_HEADS)], axis=0)
        k = k_ref[b].reshape(rows, X_HEAD_DIM).astype(BF16)
        v = v_ref[b].reshape(rows, X_HEAD_DIM).astype(BF16)
        s = lax.dot_general(q.astype(BF16), k, (((1,), (1,)), ((), ())), preferred_element_type=F32) * scale
        o = _softmax_pv(jnp.where(own_head, s, -jnp.inf), v)
        for h in range(X_HEADS):
            o_ref[b, :, h * X_HEAD_DIM:(h + 1) * X_HEAD_DIM] = o[h * T:(h + 1) * T]


def _attention_cache(q, cache_k, cache_v, bb):
    B, T, D = q.shape
    kv_spec = lambda: pl.BlockSpec((bb, N_MEM, X_HEADS, X_HEAD_DIM), lambda b: (b, 0, 0, 0))
    return pl.pallas_call(
        functools.partial(_attn_cache_kernel, bb),
        grid=(B // bb,),
        in_specs=[pl.BlockSpec((bb, T, D), lambda b: (b, 0, 0)), kv_spec(), kv_spec()],
        out_specs=pl.BlockSpec((bb, T, D), lambda b: (b, 0, 0)),
        out_shape=jax.ShapeDtypeStruct((B, T, D), F32),
        compiler_params=_cparams(("arbitrary",)),
        name="cache_attention",
    )(q, cache_k, cache_v)


def _xattn_block_kernel(x_ref, k_ref, v_ref, wq_ref, wo_ref, g_ref, b_ref, o_ref, kb_ref, vb_ref):
    @pl.when(pl.program_id(1) == 0)
    def _():
        kb_ref[...] = k_ref[0].astype(BF16)
        vb_ref[...] = v_ref[0].astype(BF16)

    scale = X_HEAD_DIM ** -0.5
    x = x_ref[0]
    q = jnp.dot(x.astype(BF16), wq_ref[...], preferred_element_type=F32)
    heads = []
    for h in range(X_HEADS):
        cols = slice(h * X_HEAD_DIM, (h + 1) * X_HEAD_DIM)
        s = lax.dot_general(q[:, cols].astype(BF16), kb_ref[:, cols], (((1,), (1,)), ((), ())),
                            preferred_element_type=F32) * scale
        heads.append(_softmax_pv(s, vb_ref[:, cols]).astype(BF16))
    att = jnp.concatenate(heads, axis=1)
    y = ALPHA * x + jnp.dot(att, wo_ref[...], preferred_element_type=F32)
    o_ref[0] = _layer_norm(y, g_ref[...], b_ref[...])


def _xattn_block(x, mem_k, mem_v, wq, wo, g, b, tq):
    B, T, D = x.shape
    resident = lambda: pl.BlockSpec((D, D), lambda i, t: (0, 0), pipeline_mode=pl.Buffered(1))
    kv = lambda: pl.BlockSpec((1, N_MEM, D), lambda i, t: (i, 0, 0))
    row = lambda: pl.BlockSpec((1, D), lambda i, t: (0, 0))
    return pl.pallas_call(
        _xattn_block_kernel,
        grid=(B, T // tq),
        in_specs=[pl.BlockSpec((1, tq, D), lambda i, t: (i, t, 0)), kv(), kv(), resident(), resident(), row(), row()],
        out_specs=pl.BlockSpec((1, tq, D), lambda i, t: (i, t, 0)),
        out_shape=jax.ShapeDtypeStruct((B, T, D), F32),
        scratch_shapes=[pltpu.VMEM((N_MEM, D), BF16), pltpu.VMEM((N_MEM, D), BF16)],
        compiler_params=_cparams(("arbitrary", "arbitrary")),
        name="xattn_block",
    )(x, mem_k, mem_v, wq, wo, g, b)


def _softplus(x):
    return jnp.maximum(x, 0.0) + jnp.log1p(jnp.exp(-jnp.abs(x)))


def _lru_kernel(bb, tc, ux_ref, ug_ref, h0_ref, c0_ref, cw_ref, cb_ref, wa_ref, ba_ref, wx_ref, bx_ref,
                lam_ref, y_ref, hl_ref, cn_ref, ext_ref, hc_ref, a_ref, b_ref, hs_ref):
    ti = pl.program_id(1)
    pad = SUBLANES
    neg_c_sp = -LRU_C * _softplus(-lam_ref[...])

    for b in range(bb):
        @pl.when(ti == 0)
        def _():
            ext_ref[b, pad - 3:pad, :] = c0_ref[b]
            hc_ref[b] = h0_ref[b]

        ext_ref[b, pad:pad + tc, :] = ux_ref[b]
        xc = cb_ref[...] + cw_ref[0:1, :] * ext_ref[b, pad - 3:pad - 3 + tc, :]
        for j in range(1, CONV_WIDTH):
            xc = xc + cw_ref[j:j + 1, :] * ext_ref[b, pad - 3 + j:pad - 3 + j + tc, :]

        ga, gx = [], []
        for n in range(LRU_BLOCKS):
            xb = xc[:, n * LRU_BLOCK:(n + 1) * LRU_BLOCK].astype(BF16)
            ga.append(jnp.dot(xb, wa_ref[n], preferred_element_type=F32))
            gx.append(jnp.dot(xb, wx_ref[n], preferred_element_type=F32))
        r = jax.nn.sigmoid(jnp.concatenate(ga, axis=1) + ba_ref[...])
        i = jax.nn.sigmoid(jnp.concatenate(gx, axis=1) + bx_ref[...])
        log_a = neg_c_sp * r
        rs = slice(b * tc, (b + 1) * tc)
        a_ref[rs, :] = jnp.exp(log_a)
        b_ref[rs, :] = jnp.sqrt(1.0 - jnp.exp(2.0 * log_a)) * (i * xc)
        tail = ext_ref[b, pad + tc - 3:pad + tc, :]
        cn_ref[b] = tail
        ext_ref[b, pad - 3:pad, :] = tail

    def step(t, hs):
        out = []
        for b in range(bb):
            row = pl.ds(b * tc + t, 1)
            h = a_ref[row, :] * hs[b] + b_ref[row, :]
            hs_ref[row, :] = h
            out.append(h)
        return tuple(out)

    hs = lax.fori_loop(0, tc, step, tuple(hc_ref[b] for b in range(bb)))
    for b in range(bb):
        hc_ref[b] = hs[b]
        hl_ref[b] = hs[b]
        y_ref[b] = hs_ref[b * tc:(b + 1) * tc, :] * jax.nn.gelu(ug_ref[b])


def _lru(p3, h0, conv0, cw, cb, wa, ba, wx, bx, lam, bb, tc):
    B, T, _ = p3.shape
    W = LRU_WIDTH
    row = lambda: pl.BlockSpec((1, W), lambda b, t: (0, 0))
    blk = lambda: pl.BlockSpec((LRU_BLOCKS, LRU_BLOCK, LRU_BLOCK), lambda b, t: (0, 0, 0))
    return pl.pallas_call(
        functools.partial(_lru_kernel, bb, tc),
        grid=(B // bb, T // tc),
        in_specs=[pl.BlockSpec((bb, tc, W), lambda b, t: (b, t, 0)),
                  pl.BlockSpec((bb, tc, W), lambda b, t: (b, t, 1)),
                  pl.BlockSpec((bb, 1, W), lambda b, t: (b, 0, 0)),
                  pl.BlockSpec((bb, CONV_WIDTH - 1, W), lambda b, t: (b, 0, 0)),
                  pl.BlockSpec((CONV_WIDTH, W), lambda b, t: (0, 0)),
                  row(), blk(), row(), blk(), row(), row()],
        out_specs=[pl.BlockSpec((bb, tc, W), lambda b, t: (b, t, 0)),
                   pl.BlockSpec((bb, 1, W), lambda b, t: (b, 0, 0)),
                   pl.BlockSpec((bb, CONV_WIDTH - 1, W), lambda b, t: (b, 0, 0))],
        out_shape=[jax.ShapeDtypeStruct((B, T, W), F32),
                   jax.ShapeDtypeStruct((B, 1, W), F32),
                   jax.ShapeDtypeStruct((B, CONV_WIDTH - 1, W), F32)],
        scratch_shapes=[pltpu.VMEM((bb, tc + SUBLANES, W), F32),
                        pltpu.VMEM((bb, 1, W), F32),
                        pltpu.VMEM((bb * tc, W), F32),
                        pltpu.VMEM((bb * tc, W), F32),
                        pltpu.VMEM((bb * tc, W), F32)],
        compiler_params=_cparams(("arbitrary", "arbitrary")),
        name="rg_lru",
    )(p3, p3, h0, conv0, cw, cb, wa, ba, wx, bx, lam)


def _head_ones(n):
    ri = lax.broadcasted_iota(jnp.int32, (n, n), 0)
    ci = lax.broadcasted_iota(jnp.int32, (n, n), 1)
    return ((ri >> 6) == (ci >> 6)).astype(BF16)


def _split3(x):
    hi = x.astype(BF16)
    r1 = x - hi.astype(F32)
    mid = r1.astype(BF16)
    return hi, mid, (r1 - mid.astype(F32)).astype(BF16)


def _head_sum(x, ones):
    return sum(jnp.dot(t, ones, preferred_element_type=F32) for t in _split3(x))


def _dot_nt(a, b):
    return lax.dot_general(a, b, (((1,), (1,)), ((), ())), preferred_element_type=F32)


def _rwkv_kernel(bb, tc, ur_ref, uk_ref, uv_ref, ul_ref, sh0_ref, s0_ref, mu_ref, w0_ref, ww_ref, a0_ref,
                 wa_ref, wg_ref, kk_ref, ka_ref, rk_ref, lnw_ref, lnb_ref,
                 y_ref, st_ref,
                 s_ref, carry_ref, kap_ref, kapa_ref, w_ref, km_ref, v_ref, r_ref, gate_ref, ob_ref,
                 u_ref, kt_ref, bt_ref, rhm_ref, av_ref, au_ref):
    ti = pl.program_id(1)
    ones = _head_ones(LANES)
    ones2 = _head_ones(2 * LANES)
    W = RWKV_WIDTH
    NH = RWKV_HEAD
    bulk_o = tc == NH
    row0 = lax.broadcasted_iota(jnp.int32, (tc, 1), 0) == 0
    sub_i = lax.broadcasted_iota(jnp.int32, (NH, LANES), 0)
    lane_i = lax.broadcasted_iota(jnp.int32, (NH, LANES), 1)
    lane_t = lane_i & (NH - 1)
    left = lane_i < NH
    eye16 = (lane_t == sub_i).astype(BF16)
    gb = 2 if bb % 2 == 0 else 1

    def lanes(c):
        return slice(c * LANES, (c + 1) * LANES)

    def rows(b):
        return slice(b * tc, (b + 1) * tc)

    @pl.when(ti == 0)
    def _():
        for b in range(bb):
            for c in range(HEAD_PAIRS):
                s_ref[b, c, :, 0:NH] = s0_ref[b, 2 * c]
                s_ref[b, c, :, NH:2 * NH] = s0_ref[b, 2 * c + 1]
        carry_ref[...] = sh0_ref[...]
        ob_ref[...] = jnp.zeros_like(ob_ref)

    def shifted(u, b, lo):
        cols = slice(lo, lo + u.shape[1])
        prev = jnp.where(row0, carry_ref[b, :, cols], pltpu.roll(u, shift=1, axis=0))
        return u + (prev - u) * mu_ref[:, cols]

    def head_sums(x):
        return jnp.concatenate([_head_sum(x[:, lanes(c)], ones) for c in range(HEAD_PAIRS)], axis=1)

    zs = [[], [], [], []]
    for b in range(bb):
        for z, src, lo in zip(zs, (ur_ref, uk_ref, uv_ref, ul_ref), (0, W, 2 * W, 3 * W)):
            z.append(shifted(src[b], b, lo))
    for b in range(bb):
        for src, lo in zip((ur_ref, uk_ref, uv_ref, ul_ref), (0, W, 2 * W, 3 * W)):
            carry_ref[b, :, lo:lo + src.shape[2]] = src[b, tc - 1:tc, :]
    zr, zk, zv, zl = (jnp.concatenate(z, axis=0) for z in zs)

    z_wa = zl[:, 0:LORA_WA_PAD]
    z_g = zl[:, LORA_WA_PAD:LORA_WA_PAD + LORA_G_PAD]
    dw = jnp.dot(jnp.tanh(z_wa).astype(BF16), ww_ref[...], preferred_element_type=F32)
    da = jnp.dot(z_wa.astype(BF16), wa_ref[...], preferred_element_type=F32)
    gate_ref[...] = jnp.dot(jax.nn.sigmoid(z_g).astype(BF16), wg_ref[...], preferred_element_type=F32)
    log_w = -DECAY_SCALE * jax.nn.sigmoid(w0_ref[...] + dw)
    a = jax.nn.sigmoid(a0_ref[...] + da)
    kk = zk * kk_ref[...]
    kap = kk * lax.rsqrt(jnp.maximum(head_sums(kk * kk), 1e-24))
    km = zk * (1.0 + (a - 1.0) * ka_ref[...])
    km_ref[...] = km
    v_ref[...] = zv
    r_ref[...] = zr

    def head_rows(x):
        return jnp.concatenate([jnp.where(left, x, 0.0), jnp.where(left, 0.0, x)], axis=0).astype(BF16)

    if not bulk_o:
        kap_ref[...] = kap
        kapa_ref[...] = kap * a
        w_ref[...] = jnp.exp(log_w)
    else:
        tri = (lax.broadcasted_iota(jnp.int32, (tc, tc), 0)
               >= lax.broadcasted_iota(jnp.int32, (tc, tc), 1)).astype(BF16)
        cum = jnp.concatenate([sum(jnp.dot(tri, t, preferred_element_type=F32) for t in _split3(log_w[rows(b)]))
                               for b in range(bb)], axis=0)
        inv_gamma = jnp.exp(-cum)
        gamma = jnp.exp(cum)
        rt = zr * gamma
        w_ref[...] = gamma
        kap_ref[...] = kap * jnp.exp(cum - log_w)
        kt_ref[...] = km * inv_gamma
        bt_ref[...] = kap * a * inv_gamma
        for b in range(bb):
            for c in range(HEAD_PAIRS):
                rhm = head_rows(rt[rows(b), lanes(c)])
                rhm_ref[b, c] = rhm
                ob_ref[b, c] = _dot_nt(s_ref[b, c].astype(BF16), rhm)

    def out_products(t_out, with_update, t):
        omask = lane_t == t_out
        for g0 in range(0, bb, gb):
            group = range(g0, g0 + gb)
            lhs, rowvecs = [], []
            for b in group:
                if not bulk_o:
                    r_p = r_ref[pl.ds(b * tc + jnp.maximum(t_out, 0), 1), :]
                if with_update:
                    row = pl.ds(b * tc + t, 1)
                    kap_t, v_t = kap_ref[row, :], v_ref[row, :]
                    if bulk_o:
                        rowvecs.append((bt_ref[row, :], kt_ref[row, :]))
                    else:
                        rowvecs.append((kapa_ref[row, :], w_ref[row, :], km_ref[row, :]))
                    for c0 in range(0, HEAD_PAIRS, 2):
                        dv, ps = [], []
                        for c in (c0, c0 + 1):
                            s = s_ref[b, c]
                            ps.append((s * kap_t[:, lanes(c)]).astype(BF16))
                            if not bulk_o:
                                lhs.append(jnp.concatenate([ps.pop(), (s * r_p[:, lanes(c)]).astype(BF16)], axis=1))
                            dv.append(eye16 * jnp.broadcast_to(v_t[:, lanes(c)], (NH, LANES)).astype(BF16))
                        if bulk_o:
                            lhs.append(jnp.concatenate(ps, axis=1))
                        lhs.append(jnp.concatenate(dv, axis=1))
                else:
                    for c in range(0, HEAD_PAIRS, 2):
                        lhs.append(jnp.concatenate([(s_ref[b, c] * r_p[:, lanes(c)]).astype(BF16),
                                                    (s_ref[b, c + 1] * r_p[:, lanes(c + 1)]).astype(BF16)], axis=1))
            red = jnp.dot(jnp.concatenate(lhs, axis=0), ones2, preferred_element_type=F32)
            for bi, b in enumerate(group):
                for c in range(HEAD_PAIRS):
                    half = slice((c % 2) * LANES, (c % 2 + 1) * LANES)
                    if with_update and bulk_o:
                        base = (bi * (HEAD_PAIRS // 2) + c // 2) * 2 * NH
                        skk = red[base:base + NH, half]
                        vb = red[base + NH:base + 2 * NH, half]
                    elif with_update:
                        base = ((bi * (HEAD_PAIRS // 2) + c // 2) * 3 + c % 2) * NH
                        skk = red[base:base + NH, 0:LANES]
                        ob = red[base:base + NH, LANES:2 * LANES]
                        vbase = ((bi * (HEAD_PAIRS // 2) + c // 2) * 3 + 2) * NH
                        vb = red[vbase:vbase + NH, half]
                    else:
                        base = (bi * (HEAD_PAIRS // 2) + c // 2) * NH
                        ob = red[base:base + NH, half]
                    if with_update and bulk_o:
                        bt_t, kt_t = rowvecs[bi]
                        s_ref[b, c] = s_ref[b, c] - skk * bt_t[:, lanes(c)] + vb * kt_t[:, lanes(c)]
                    elif with_update:
                        kapa_t, w_t, km_t = rowvecs[bi]
                        s_ref[b, c] = (s_ref[b, c] * w_t[:, lanes(c)] - skk * kapa_t[:, lanes(c)]
                                       + vb * km_t[:, lanes(c)])
                    if bulk_o:
                        pltpu.store(u_ref.at[b, c], skk, mask=omask)
                    else:
                        pltpu.store(ob_ref.at[b, c], ob, mask=omask)

    if bulk_o:
        def step(t, carry):
            out_products(t, True, t)
            return carry

        lax.fori_loop(0, tc, step, 0, unroll=8)
        for b in range(bb):
            for c in range(HEAD_PAIRS):
                s_ref[b, c] = s_ref[b, c] * w_ref[(b + 1) * tc - 1:(b + 1) * tc, lanes(c)]
    else:
        def step(t, carry):
            out_products(t - 1, True, t)
            return carry

        lax.fori_loop(0, tc, step, 0, unroll=8)
        out_products(tc - 1, False, None)

    for b in range(bb):
        for c in range(HEAD_PAIRS):
            st_ref[b, 2 * c] = s_ref[b, c, :, 0:NH]
            st_ref[b, 2 * c + 1] = s_ref[b, c, :, NH:2 * NH]

    if bulk_o:
        rr = lax.broadcasted_iota(jnp.int32, (LANES, LANES), 0) & (NH - 1)
        cc = lax.broadcasted_iota(jnp.int32, (LANES, LANES), 1) & (NH - 1)
        causal = rr <= cc
        pad_rows = jnp.zeros((LANES - tc, LANES), F32)
        for b in range(bb):
            for c in range(HEAD_PAIRS):
                rhm = rhm_ref[b, c]
                av_ref[b, c] = jnp.where(causal, _dot_nt(head_rows(kt_ref[rows(b), lanes(c)]), rhm), 0.0).astype(BF16)
                au_ref[b, c] = jnp.where(causal, _dot_nt(head_rows(bt_ref[rows(b), lanes(c)]), rhm), 0.0).astype(BF16)
        for b in range(bb):
            for c in range(HEAD_PAIRS):
                vt = jnp.concatenate([v_ref[rows(b), lanes(c)], pad_rows], axis=0).T
                v_cols = jnp.where(left, vt[0:NH], pltpu.roll(vt[NH:2 * NH], shift=NH, axis=1))
                ob_ref[b, c] = (ob_ref[b, c]
                                + jnp.dot(v_cols.astype(BF16), av_ref[b, c], preferred_element_type=F32)
                                - jnp.dot(u_ref[b, c].astype(BF16), au_ref[b, c], preferred_element_type=F32))

    inv_n = 1.0 / RWKV_HEAD
    o_rows = []
    for b in range(bb):
        o_chunks = []
        for c in range(0, HEAD_PAIRS, 2):
            xt = jnp.concatenate([ob_ref[b, c], ob_ref[b, c + 1]], axis=0).T
            top, bot = xt[0:NH], xt[NH:2 * NH]
            o_chunks.append(jnp.where(left, top, pltpu.roll(bot, shift=NH, axis=1))[0:tc])
            o_chunks.append(jnp.where(left, pltpu.roll(top, shift=NH, axis=1), bot)[0:tc])
        o_rows.append(jnp.concatenate(o_chunks, axis=1))
    o = jnp.concatenate(o_rows, axis=0)
    mean = head_sums(o) * inv_n
    d = o - mean
    var = head_sums(d * d) * inv_n
    on = d * lax.rsqrt(var + GN_EPS) * lnw_ref[...] + lnb_ref[...]
    bonus = head_sums(r_ref[...] * km_ref[...] * rk_ref[...]) * v_ref[...]
    y = (on + bonus) * gate_ref[...]
    for b in range(bb):
        y_ref[b] = y[rows(b)]


def _rwkv(p3, shift0, s0, layer, mu, w0, ww, a0, wa, wg, kk, ka, rk, lnw, lnb, bb, tc):
    B, T, _ = p3.shape
    W = RWKV_WIDTH
    SW = 3 * W + LORA_PAD
    row = lambda: pl.BlockSpec((1, W), lambda b, t: (0, 0))
    lora = lambda rows: pl.BlockSpec((rows, W), lambda b, t: (0, 0))
    state = lambda: pl.BlockSpec((bb, RWKV_HEADS, RWKV_HEAD, RWKV_HEAD), lambda b, t: (b, 0, 0, 0))
    state_in = pl.BlockSpec((None, bb, RWKV_HEADS, RWKV_HEAD, RWKV_HEAD), lambda b, t: (layer, b, 0, 0, 0))
    chunk = lambda: pltpu.VMEM((bb * tc, W), F32)
    packed = lambda: pltpu.VMEM((bb, HEAD_PAIRS, RWKV_HEAD, LANES), F32)
    pair_mat = lambda: pltpu.VMEM((bb, HEAD_PAIRS, LANES, LANES), BF16)
    return pl.pallas_call(
        functools.partial(_rwkv_kernel, bb, tc),
        grid=(B // bb, T // tc),
        in_specs=[pl.BlockSpec((bb, tc, W), lambda b, t: (b, t, 2)),
                  pl.BlockSpec((bb, tc, W), lambda b, t: (b, t, 3)),
                  pl.BlockSpec((bb, tc, W), lambda b, t: (b, t, 4)),
                  pl.BlockSpec((bb, tc, LORA_PAD), lambda b, t: (b, t, LORA_COL_BLOCK)),
                  pl.BlockSpec((bb, 1, SW), lambda b, t: (b, 0, 0)),
                  state_in,
                  pl.BlockSpec((1, SW), lambda b, t: (0, 0)),
                  row(), lora(LORA_WA_PAD), row(), lora(LORA_WA_PAD), lora(LORA_G_PAD),
                  row(), row(), row(), row(), row()],
        out_specs=[pl.BlockSpec((bb, tc, W), lambda b, t: (b, t, 0)), state()],
        out_shape=[jax.ShapeDtypeStruct((B, T, W), F32),
                   jax.ShapeDtypeStruct((B, RWKV_HEADS, RWKV_HEAD, RWKV_HEAD), F32)],
        scratch_shapes=[packed(), pltpu.VMEM((bb, 1, SW), F32),
                        chunk(), chunk(), chunk(), chunk(), chunk(), chunk(), chunk(),
                        packed(), packed(), chunk(), chunk(), pair_mat(), pair_mat(), pair_mat()],
        compiler_params=_cparams(("arbitrary", "arbitrary")),
        name="rwkv7",
    )(p3, p3, p3, p3, shift0, s0, mu, w0, ww, a0, wa, wg, kk, ka, rk, lnw, lnb)


def _pad_shift(a):
    pad = [(0, 0)] * (a.ndim - 1) + [(0, LORA_PAD - LORA_TOTAL)]
    return jnp.pad(a, pad)


def _lora_rows(w_up, offset, rows):
    rank = w_up.shape[0]
    return jnp.pad(w_up, ((offset, rows - offset - rank), (0, 0))).astype(BF16)


def _hybrid_layer(x, mem_k, mem_v, h0, conv0, S0, shift0, wts, layer, cfg):
    B, T, D = x.shape
    M = B * T
    x2 = x.reshape(M, D)

    p = _matmul(x2, wts["w_in_t"], layer, cfg["tm_mm"], LORA_PAD, w_tail=wts["w_in_tail"], w_is_nk=True)
    p3 = p.reshape(B, T, P_PAD)

    y_lru, h_last, conv_new = _lru(p3, h0.reshape(B, 1, LRU_WIDTH), conv0,
                                   wts["conv_w"], wts["conv_b"], wts["lru_wa"], wts["lru_ba"],
                                   wts["lru_wx"], wts["lru_bx"], wts["lru_L"], cfg["lru_bb"], cfg["lru_tc"])

    y_rwkv, s_new = _rwkv(p3, _pad_shift(shift0).reshape(B, 1, -1), S0, layer,
                          wts["mu"], wts["w0"], wts["ww"], wts["a0"], wts["wa"], wts["wg"],
                          wts["k_k"], wts["k_a"], wts["r_k"], wts["ln_w"], wts["ln_b"],
                          cfg["rwkv_bb"], cfg["rwkv_tc"])
    shift_new = p3[:, T - 1, 2 * LRU_WIDTH:P_TOTAL]

    x1 = _matmul_res_ln([y_lru.reshape(M, LRU_WIDTH), y_rwkv.reshape(M, RWKV_WIDTH)],
                        [wts["w_out_a"], wts["w_out_b"]], x2, wts["ln1_g"], wts["ln1_b"], cfg["tm_ln"])

    if mem_k.ndim == 4:
        q = _matmul(x1, wts["wq"], layer, cfg["tm_mm"], 512)
        att = _attention_cache(q.reshape(B, T, D), mem_k, mem_v, cfg["att_bb"])
        x2n = _matmul_res_ln([att.reshape(M, D)], [wts["wo"]], x1, wts["ln2_g"], wts["ln2_b"], cfg["tm_ln"])
    else:
        x2n = _xattn_block(x1.reshape(B, T, D), mem_k, mem_v, wts["wq_b"], wts["wo"],
                           wts["ln2_g"], wts["ln2_b"], cfg["att_tq"]).reshape(M, D)

    x3 = _mlp_ln(x2n, wts["w1"], wts["w2"], wts["ln3_g"], wts["ln3_b"], cfg["tm"], cfg["tf"])
    return (x3.reshape(B, T, D), h_last.reshape(B, LRU_WIDTH), conv_new,
            s_new, shift_new)


def kernel(x_prompt, x_sample, mem_prompt, cache_mem_k, cache_mem_v, state_lru_h, state_lru_conv, state_rwkv_S, state_rwkv_shift, w_in, lru_conv_w, lru_conv_b, lru_wa, lru_ba, lru_wx, lru_bx, lru_L, rwkv_mu, rwkv_w0, rwkv_w_up, rwkv_a0, rwkv_a_up, rwkv_g_up, rwkv_k_k, rwkv_k_a, rwkv_r_k, rwkv_ln_w, rwkv_ln_b, w_out, ln1_g, ln1_b, xa_wq, xa_wk, xa_wv, xa_wo, ln2_g, ln2_b, mlp_w1, mlp_w2, ln3_g, ln3_b):
    B, T, D = x_prompt.shape
    Bs, Ts, _ = x_sample.shape
    assert w_in.shape[0] == DEPTH

    cfg_p = dict(tm_mm=1024, tm=512, tf=1024, tm_ln=256, lru_bb=2, lru_tc=256, rwkv_bb=B, rwkv_tc=64,
                 att_bb=1, att_tq=256)
    cfg_s = dict(tm_mm=1024, tm=512, tf=1024, tm_ln=256, lru_bb=8, lru_tc=Ts, rwkv_bb=8, rwkv_tc=Ts,
                 att_bb=2, att_tq=Ts)

    yp, ys = x_prompt, x_sample
    outs_p = [[] for _ in range(6)]
    outs_s = [[] for _ in range(4)]
    for l in range(DEPTH):
        row = lambda a: a[l].reshape(1, -1)
        wts = dict(
            w_in_t=jnp.swapaxes(w_in, 1, 2),
            w_in_tail=jnp.pad(jnp.swapaxes(w_in, 1, 2)[l, P_PAD - LORA_PAD:],
                              ((0, P_PAD - P_TOTAL), (0, 0))).astype(BF16),
            conv_w=lru_conv_w[l], conv_b=row(lru_conv_b),
            lru_wa=lru_wa[l].astype(BF16), lru_ba=row(lru_ba),
            lru_wx=lru_wx[l].astype(BF16), lru_bx=row(lru_bx), lru_L=row(lru_L),
            mu=_pad_shift(rwkv_mu[l]).reshape(1, -1), w0=row(rwkv_w0), a0=row(rwkv_a0),
            ww=_lora_rows(rwkv_w_up[l], 0, LORA_WA_PAD), wa=_lora_rows(rwkv_a_up[l], DECAY_LORA, LORA_WA_PAD),
            wg=_lora_rows(rwkv_g_up[l], 0, LORA_G_PAD),
            k_k=row(rwkv_k_k), k_a=row(rwkv_k_a), r_k=row(rwkv_r_k), ln_w=row(rwkv_ln_w), ln_b=row(rwkv_ln_b),
            w_out_a=w_out[l, :LRU_WIDTH].astype(BF16), w_out_b=w_out[l, LRU_WIDTH:].astype(BF16),
            ln1_g=row(ln1_g), ln1_b=row(ln1_b),
            wq=xa_wq, wq_b=xa_wq[l].astype(BF16), wo=xa_wo[l].astype(BF16), ln2_g=row(ln2_g), ln2_b=row(ln2_b),
            w1=mlp_w1[l].astype(BF16), w2=mlp_w2[l].astype(BF16), ln3_g=row(ln3_g), ln3_b=row(ln3_b),
        )
        mem2 = mem_prompt.reshape(B * N_MEM, D)
        mem_k = _matmul(mem2, xa_wk, l, 512, 512).reshape(B, N_MEM, D)
        mem_v = _matmul(mem2, xa_wv, l, 512, 512).reshape(B, N_MEM, D)
        yp, hl, cl, Sl, shl = _hybrid_layer(
            yp, mem_k, mem_v,
            jnp.zeros((B, LRU_WIDTH), F32), jnp.zeros((B, CONV_WIDTH - 1, LRU_WIDTH), F32),
            jnp.zeros((1, B, RWKV_HEADS, RWKV_HEAD, RWKV_HEAD), F32), jnp.zeros((B, SHIFT_WIDTH), F32),
            wts, 0, cfg_p)
        for lst, val in zip(outs_p, (mem_k.reshape(B, N_MEM, X_HEADS, X_HEAD_DIM),
                                     mem_v.reshape(B, N_MEM, X_HEADS, X_HEAD_DIM), hl, cl, Sl, shl)):
            lst.append(val)
        ys, hl, cl, Sl, shl = _hybrid_layer(
            ys, cache_mem_k[l], cache_mem_v[l],
            state_lru_h[l], state_lru_conv[l], state_rwkv_S, state_rwkv_shift[l], wts, l, cfg_s)
        for lst, val in zip(outs_s, (hl, cl, Sl, shl)):
            lst.append(val)

    return (yp, ys, *(jnp.stack(o) for o in outs_p), *(jnp.stack(o) for o in outs_s))
```

```python
import functools
import math

import jax
import jax.numpy as jnp
from jax import lax
from jax.experimental import pallas as pl
from jax.experimental.pallas import tpu as pltpu

F32 = jnp.float32
BF16 = jnp.bfloat16

D_MODEL = 2048
LRU_WIDTH = 1024
LRU_BLOCKS = 8
LRU_BLOCK = 128
CONV_WIDTH = 4
LRU_C = 8.0
RWKV_WIDTH = 1024
RWKV_HEAD = 64
RWKV_HEADS = 16
DECAY_LORA = 64
A_LORA = 64
G_LORA = 160
LORA_TOTAL = DECAY_LORA + A_LORA + G_LORA
SHIFT_WIDTH = 3 * RWKV_WIDTH + LORA_TOTAL
P_TOTAL = 2 * LRU_WIDTH + SHIFT_WIDTH
DECAY_SCALE = math.exp(-0.5)
N_MEM = 256
X_HEADS = 4
X_HEAD_DIM = 512
D_FF = 4 * D_MODEL
LN_EPS = 1e-5
GN_EPS = 64e-5
DEPTH = 1
ALPHA = (2 * DEPTH) ** 0.25

LANES = 128
SUBLANES = 8
VMEM_LIMIT_BYTES = 56 * 1024 * 1024

LORA_PAD = 512
LORA_WA_PAD = DECAY_LORA + A_LORA
LORA_G_PAD = 2 * LANES
P_PAD = 5 * 1024 + LORA_PAD
LORA_COL_BLOCK = (5 * 1024) // LORA_PAD
HEAD_PAIRS = RWKV_WIDTH // LANES


def _cparams(sem):
    return pltpu.CompilerParams(dimension_semantics=sem, vmem_limit_bytes=VMEM_LIMIT_BYTES)


def _mm_kernel(n_main, has_tail, w_is_nk, x_ref, w_ref, *rest):
    if has_tail:
        wt_ref, o_ref, xb_ref = rest
    else:
        o_ref, xb_ref = rest
    j = pl.program_id(1)
    contract = (((1,), (1,)), ((), ())) if w_is_nk else (((1,), (0,)), ((), ()))

    @pl.when(j == 0)
    def _():
        xb_ref[...] = x_ref[...].astype(BF16)

    @pl.when(j < n_main)
    def _():
        o_ref[...] = lax.dot_general(xb_ref[...], w_ref[...].astype(BF16), contract, preferred_element_type=F32)

    if has_tail:
        @pl.when(j == n_main)
        def _():
            o_ref[...] = lax.dot_general(xb_ref[...], wt_ref[...], contract, preferred_element_type=F32)


def _matmul(x, w, layer, tm, tn, w_tail=None, w_is_nk=False):
    M, K = x.shape
    N = w.shape[1] if w_is_nk else w.shape[2]
    n_main = N // tn
    has_tail = w_tail is not None
    n_tiles = n_main + int(has_tail)
    assert M % tm == 0 and (has_tail or N % tn == 0)
    if w_is_nk:
        w_spec = pl.BlockSpec((None, tn, K), lambda i, j: (layer, jnp.minimum(j, n_main - 1), 0))
    else:
        w_spec = pl.BlockSpec((None, K, tn), lambda i, j: (layer, 0, jnp.minimum(j, n_main - 1)))
    in_specs = [pl.BlockSpec((tm, K), lambda i, j: (i, 0)), w_spec]
    args = [x, w]
    if has_tail:
        in_specs.append(pl.BlockSpec(w_tail.shape, lambda i, j: (0, 0)))
        args.append(w_tail)
    return pl.pallas_call(
        functools.partial(_mm_kernel, n_main, has_tail, w_is_nk),
        grid=(M // tm, n_tiles),
        in_specs=in_specs,
        out_specs=pl.BlockSpec((tm, tn), lambda i, j: (i, j)),
        out_shape=jax.ShapeDtypeStruct((M, n_tiles * tn), F32),
        scratch_shapes=[pltpu.VMEM((tm, K), BF16)],
        compiler_params=_cparams(("arbitrary", "arbitrary")),
        name="matmul",
    )(*args)


def _layer_norm(y, g, b):
    mu = jnp.mean(y, axis=-1, keepdims=True)
    d = y - mu
    var = jnp.mean(d * d, axis=-1, keepdims=True)
    return d * lax.rsqrt(var + LN_EPS) * g + b


def _mm_ln_kernel(n_in, *refs):
    x_refs = refs[:n_in]
    w_refs = refs[n_in:2 * n_in]
    res_ref, g_ref, b_ref, o_ref = refs[2 * n_in:]
    acc = jnp.dot(x_refs[0][...].astype(BF16), w_refs[0][...], preferred_element_type=F32)
    for x_ref, w_ref in zip(x_refs[1:], w_refs[1:]):
        acc = acc + jnp.dot(x_ref[...].astype(BF16), w_ref[...], preferred_element_type=F32)
    y = ALPHA * res_ref[...] + acc
    o_ref[...] = _layer_norm(y, g_ref[...], b_ref[...])


def _matmul_res_ln(xs, ws, res, g, b, tm):
    M, N = res.shape
    n_in = len(xs)
    in_specs = [pl.BlockSpec((tm, x.shape[1]), lambda i: (i, 0)) for x in xs]
    in_specs += [pl.BlockSpec(w.shape, lambda i: (0, 0)) for w in ws]
    in_specs += [pl.BlockSpec((tm, N), lambda i: (i, 0)),
                 pl.BlockSpec((1, N), lambda i: (0, 0)),
                 pl.BlockSpec((1, N), lambda i: (0, 0))]
    return pl.pallas_call(
        functools.partial(_mm_ln_kernel, n_in),
        grid=(M // tm,),
        in_specs=in_specs,
        out_specs=pl.BlockSpec((tm, N), lambda i: (i, 0)),
        out_shape=jax.ShapeDtypeStruct((M, N), F32),
        compiler_params=_cparams(("arbitrary",)),
        name="matmul_res_ln",
    )(*xs, *ws, res, g, b)


def _mlp_kernel(x_ref, w1_ref, w2_ref, g_ref, b_ref, o_ref, xb_ref, acc_ref):
    f = pl.program_id(1)

    @pl.when(f == 0)
    def _():
        xb_ref[...] = x_ref[...].astype(BF16)
        acc_ref[...] = jnp.zeros_like(acc_ref)

    h = jnp.dot(xb_ref[...], w1_ref[...], preferred_element_type=F32)
    h = jnp.square(jnp.maximum(h, 0.0))
    acc_ref[...] += jnp.dot(h.astype(BF16), w2_ref[...], preferred_element_type=F32)

    @pl.when(f == pl.num_programs(1) - 1)
    def _():
        y = ALPHA * x_ref[...] + acc_ref[...]
        o_ref[...] = _layer_norm(y, g_ref[...], b_ref[...])


def _mlp_ln(x, w1, w2, g, b, tm, tf):
    M, D = x.shape
    FF = w1.shape[1]
    return pl.pallas_call(
        _mlp_kernel,
        grid=(M // tm, FF // tf),
        in_specs=[pl.BlockSpec((tm, D), lambda i, f: (i, 0)),
                  pl.BlockSpec((D, tf), lambda i, f: (0, f)),
                  pl.BlockSpec((tf, D), lambda i, f: (f, 0)),
                  pl.BlockSpec((1, D), lambda i, f: (0, 0)),
                  pl.BlockSpec((1, D), lambda i, f: (0, 0))],
        out_specs=pl.BlockSpec((tm, D), lambda i, f: (i, 0)),
        out_shape=jax.ShapeDtypeStruct((M, D), F32),
        scratch_shapes=[pltpu.VMEM((tm, D), BF16), pltpu.VMEM((tm, D), F32)],
        compiler_params=_cparams(("arbitrary", "arbitrary")),
        name="mlp_ln",
    )(x, w1, w2, g, b)


def _softmax_pv(s, v):
    m = jnp.max(s, axis=-1, keepdims=True)
    e = jnp.exp(s - m)
    p = e / jnp.sum(e, axis=-1, keepdims=True)
    return jnp.dot(p.astype(BF16), v, preferred_element_type=F32)


def _attn_cache_kernel(bb, q_ref, k_ref, v_ref, o_ref):
    scale = X_HEAD_DIM ** -0.5
    T = q_ref.shape[1]
    rows = N_MEM * X_HEADS
    q_head = lax.broadcasted_iota(jnp.int32, (X_HEADS * T, rows), 0) // T
    k_head = lax.broadcasted_iota(jnp.int32, (X_HEADS * T, rows), 1) & (X_HEADS - 1)
    own_head = q_head == k_head
    for b in range(bb):
        q = jnp.concatenate([q_ref[b, :, h * X_HEAD_DIM:(h + 1) * X_HEAD_DIM] for h in range(X_HEADS)], axis=0)
        k = k_ref[b].reshape(rows, X_HEAD_DIM).astype(BF16)
        v = v_ref[b].reshape(rows, X_HEAD_DIM).astype(BF16)
        s = lax.dot_general(q.astype(BF16), k, (((1,), (1,)), ((), ())), preferred_element_type=F32) * scale
        o = _softmax_pv(jnp.where(own_head, s, -jnp.inf), v)
        for h in range(X_HEADS):
            o_ref[b, :, h * X_HEAD_DIM:(h + 1) * X_HEAD_DIM] = o[h * T:(h + 1) * T]


def _attention_cache(q, cache_k, cache_v, bb):
    B, T, D = q.shape
    kv_spec = lambda: pl.BlockSpec((bb, N_MEM, X_HEADS, X_HEAD_DIM), lambda b: (b, 0, 0, 0))
    return pl.pallas_call(
        functools.partial(_attn_cache_kernel, bb),
        grid=(B // bb,),
        in_specs=[pl.BlockSpec((bb, T, D), lambda b: (b, 0, 0)), kv_spec(), kv_spec()],
        out_specs=pl.BlockSpec((bb, T, D), lambda b: (b, 0, 0)),
        out_shape=jax.ShapeDtypeStruct((B, T, D), F32),
        compiler_params=_cparams(("arbitrary",)),
        name="cache_attention",
    )(q, cache_k, cache_v)


def _xattn_block_kernel(x_ref, k_ref, v_ref, wq_ref, wo_ref, g_ref, b_ref, o_ref, kb_ref, vb_ref):
    @pl.when(pl.program_id(1) == 0)
    def _():
        kb_ref[...] = k_ref[0].astype(BF16)
        vb_ref[...] = v_ref[0].astype(BF16)

    scale = X_HEAD_DIM ** -0.5
    x = x_ref[0]
    q = jnp.dot(x.astype(BF16), wq_ref[...], preferred_element_type=F32)
    heads = []
    for h in range(X_HEADS):
        cols = slice(h * X_HEAD_DIM, (h + 1) * X_HEAD_DIM)
        s = lax.dot_general(q[:, cols].astype(BF16), kb_ref[:, cols], (((1,), (1,)), ((), ())),
                            preferred_element_type=F32) * scale
        heads.append(_softmax_pv(s, vb_ref[:, cols]).astype(BF16))
    att = jnp.concatenate(heads, axis=1)
    y = ALPHA * x + jnp.dot(att, wo_ref[...], preferred_element_type=F32)
    o_ref[0] = _layer_norm(y, g_ref[...], b_ref[...])


def _xattn_block(x, mem_k, mem_v, wq, wo, g, b, tq):
    B, T, D = x.shape
    resident = lambda: pl.BlockSpec((D, D), lambda i, t: (0, 0), pipeline_mode=pl.Buffered(1))
    kv = lambda: pl.BlockSpec((1, N_MEM, D), lambda i, t: (i, 0, 0))
    row = lambda: pl.BlockSpec((1, D), lambda i, t: (0, 0))
    return pl.pallas_call(
        _xattn_block_kernel,
        grid=(B, T // tq),
        in_specs=[pl.BlockSpec((1, tq, D), lambda i, t: (i, t, 0)), kv(), kv(), resident(), resident(), row(), row()],
        out_specs=pl.BlockSpec((1, tq, D), lambda i, t: (i, t, 0)),
        out_shape=jax.ShapeDtypeStruct((B, T, D), F32),
        scratch_shapes=[pltpu.VMEM((N_MEM, D), BF16), pltpu.VMEM((N_MEM, D), BF16)],
        compiler_params=_cparams(("arbitrary", "arbitrary")),
        name="xattn_block",
    )(x, mem_k, mem_v, wq, wo, g, b)


def _softplus(x):
    return jnp.maximum(x, 0.0) + jnp.log1p(jnp.exp(-jnp.abs(x)))


def _lru_kernel(bb, tc, ux_ref, ug_ref, h0_ref, c0_ref, cw_ref, cb_ref, wa_ref, ba_ref, wx_ref, bx_ref,
                lam_ref, y_ref, hl_ref, cn_ref, ext_ref, hc_ref, a_ref, b_ref, hs_ref):
    ti = pl.program_id(1)
    pad = SUBLANES
    neg_c_sp = -LRU_C * _softplus(-lam_ref[...])

    for b in range(bb):
        @pl.when(ti == 0)
        def _():
            ext_ref[b, pad - 3:pad, :] = c0_ref[b]
            hc_ref[b] = h0_ref[b]

        ext_ref[b, pad:pad + tc, :] = ux_ref[b]
        xc = cb_ref[...] + cw_ref[0:1, :] * ext_ref[b, pad - 3:pad - 3 + tc, :]
        for j in range(1, CONV_WIDTH):
            xc = xc + cw_ref[j:j + 1, :] * ext_ref[b, pad - 3 + j:pad - 3 + j + tc, :]

        ga, gx = [], []
        for n in range(LRU_BLOCKS):
            xb = xc[:, n * LRU_BLOCK:(n + 1) * LRU_BLOCK].astype(BF16)
            ga.append(jnp.dot(xb, wa_ref[n], preferred_element_type=F32))
            gx.append(jnp.dot(xb, wx_ref[n], preferred_element_type=F32))
        r = jax.nn.sigmoid(jnp.concatenate(ga, axis=1) + ba_ref[...])
        i = jax.nn.sigmoid(jnp.concatenate(gx, axis=1) + bx_ref[...])
        log_a = neg_c_sp * r
        rs = slice(b * tc, (b + 1) * tc)
        a_ref[rs, :] = jnp.exp(log_a)
        b_ref[rs, :] = jnp.sqrt(1.0 - jnp.exp(2.0 * log_a)) * (i * xc)
        tail = ext_ref[b, pad + tc - 3:pad + tc, :]
        cn_ref[b] = tail
        ext_ref[b, pad - 3:pad, :] = tail

    def step(t, hs):
        out = []
        for b in range(bb):
            row = pl.ds(b * tc + t, 1)
            h = a_ref[row, :] * hs[b] + b_ref[row, :]
            hs_ref[row, :] = h
            out.append(h)
        return tuple(out)

    hs = lax.fori_loop(0, tc, step, tuple(hc_ref[b] for b in range(bb)))
    for b in range(bb):
        hc_ref[b] = hs[b]
        hl_ref[b] = hs[b]
        y_ref[b] = hs_ref[b * tc:(b + 1) * tc, :] * jax.nn.gelu(ug_ref[b])


def _lru(p3, h0, conv0, cw, cb, wa, ba, wx, bx, lam, bb, tc):
    B, T, _ = p3.shape
    W = LRU_WIDTH
    row = lambda: pl.BlockSpec((1, W), lambda b, t: (0, 0))
    blk = lambda: pl.BlockSpec((LRU_BLOCKS, LRU_BLOCK, LRU_BLOCK), lambda b, t: (0, 0, 0))
    return pl.pallas_call(
        functools.partial(_lru_kernel, bb, tc),
        grid=(B // bb, T // tc),
        in_specs=[pl.BlockSpec((bb, tc, W), lambda b, t: (b, t, 0)),
                  pl.BlockSpec((bb, tc, W), lambda b, t: (b, t, 1)),
                  pl.BlockSpec((bb, 1, W), lambda b, t: (b, 0, 0)),
                  pl.BlockSpec((bb, CONV_WIDTH - 1, W), lambda b, t: (b, 0, 0)),
                  pl.BlockSpec((CONV_WIDTH, W), lambda b, t: (0, 0)),
                  row(), blk(), row(), blk(), row(), row()],
        out_specs=[pl.BlockSpec((bb, tc, W), lambda b, t: (b, t, 0)),
                   pl.BlockSpec((bb, 1, W), lambda b, t: (b, 0, 0)),
                   pl.BlockSpec((bb, CONV_WIDTH - 1, W), lambda b, t: (b, 0, 0))],
        out_shape=[jax.ShapeDtypeStruct((B, T, W), F32),
                   jax.ShapeDtypeStruct((B, 1, W), F32),
                   jax.ShapeDtypeStruct((B, CONV_WIDTH - 1, W), F32)],
        scratch_shapes=[pltpu.VMEM((bb, tc + SUBLANES, W), F32),
                        pltpu.VMEM((bb, 1, W), F32),
                        pltpu.VMEM((bb * tc, W), F32),
                        pltpu.VMEM((bb * tc, W), F32),
                        pltpu.VMEM((bb * tc, W), F32)],
        compiler_params=_cparams(("arbitrary", "arbitrary")),
        name="rg_lru",
    )(p3, p3, h0, conv0, cw, cb, wa, ba, wx, bx, lam)


def _head_ones(n):
    ri = lax.broadcasted_iota(jnp.int32, (n, n), 0)
    ci = lax.broadcasted_iota(jnp.int32, (n, n), 1)
    return ((ri >> 6) == (ci >> 6)).astype(BF16)


def _split3(x):
    hi = x.astype(BF16)
    r1 = x - hi.astype(F32)
    mid = r1.astype(BF16)
    return hi, mid, (r1 - mid.astype(F32)).astype(BF16)


def _head_sum(x, ones):
    return sum(jnp.dot(t, ones, preferred_element_type=F32) for t in _split3(x))


def _dot_nt(a, b):
    return lax.dot_general(a, b, (((1,), (1,)), ((), ())), preferred_element_type=F32)


def _rwkv_kernel(bb, tc, ur_ref, uk_ref, uv_ref, ul_ref, sh0_ref, s0_ref, mu_ref, w0_ref, ww_ref, a0_ref,
                 wa_ref, wg_ref, kk_ref, ka_ref, rk_ref, lnw_ref, lnb_ref,
                 y_ref, st_ref,
                 s_ref, carry_ref, kap_ref, kapa_ref, w_ref, km_ref, v_ref, r_ref, gate_ref, ob_ref,
                 u_ref, g1_ref, bt_ref, rhm_ref, khm_ref, av_ref, au_ref, vc_ref):
    ti = pl.program_id(1)
    ones = _head_ones(LANES)
    ones2 = _head_ones(2 * LANES)
    W = RWKV_WIDTH
    NH = RWKV_HEAD
    bulk_o = tc == NH
    row0 = lax.broadcasted_iota(jnp.int32, (tc, 1), 0) == 0
    sub_i = lax.broadcasted_iota(jnp.int32, (NH, LANES), 0)
    lane_i = lax.broadcasted_iota(jnp.int32, (NH, LANES), 1)
    lane_t = lane_i & (NH - 1)
    left = lane_i < NH
    eye16 = (lane_t == sub_i).astype(BF16)
    gb = 2 if bb % 2 == 0 else 1

    def lanes(c):
        return slice(c * LANES, (c + 1) * LANES)

    def rows(b):
        return slice(b * tc, (b + 1) * tc)

    @pl.when(ti == 0)
    def _():
        for b in range(bb):
            for c in range(HEAD_PAIRS):
                s_ref[b, c, :, 0:NH] = s0_ref[b, 2 * c]
                s_ref[b, c, :, NH:2 * NH] = s0_ref[b, 2 * c + 1]
        carry_ref[...] = sh0_ref[...]
        ob_ref[...] = jnp.zeros_like(ob_ref)

    def shifted(u, b, lo):
        cols = slice(lo, lo + u.shape[1])
        prev = jnp.where(row0, carry_ref[b, :, cols], pltpu.roll(u, shift=1, axis=0))
        return u + (prev - u) * mu_ref[:, cols]

    def head_sums(x):
        return jnp.concatenate([_head_sum(x[:, lanes(c)], ones) for c in range(HEAD_PAIRS)], axis=1)

    zs = [[], [], [], []]
    for b in range(bb):
        for z, src, lo in zip(zs, (ur_ref, uk_ref, uv_ref, ul_ref), (0, W, 2 * W, 3 * W)):
            z.append(shifted(src[b], b, lo))
    for b in range(bb):
        for src, lo in zip((ur_ref, uk_ref, uv_ref, ul_ref), (0, W, 2 * W, 3 * W)):
            carry_ref[b, :, lo:lo + src.shape[2]] = src[b, tc - 1:tc, :]
    zr, zk, zv, zl = (jnp.concatenate(z, axis=0) for z in zs)

    z_wa = zl[:, 0:LORA_WA_PAD]
    z_g = zl[:, LORA_WA_PAD:LORA_WA_PAD + LORA_G_PAD]
    dw = jnp.dot(jnp.tanh(z_wa).astype(BF16), ww_ref[...], preferred_element_type=F32)
    da = jnp.dot(z_wa.astype(BF16), wa_ref[...], preferred_element_type=F32)
    gate_ref[...] = jnp.dot(jax.nn.sigmoid(z_g).astype(BF16), wg_ref[...], preferred_element_type=F32)
    log_w = -DECAY_SCALE * jax.nn.sigmoid(w0_ref[...] + dw)
    a = jax.nn.sigmoid(a0_ref[...] + da)
    kk = zk * kk_ref[...]
    kap = kk * lax.rsqrt(jnp.maximum(head_sums(kk * kk), 1e-24))
    km = zk * (1.0 + (a - 1.0) * ka_ref[...])
    km_ref[...] = km
    v_ref[...] = zv
    r_ref[...] = zr

    def head_rows(x):
        return jnp.concatenate([jnp.where(left, x, 0.0), jnp.where(left, 0.0, x)], axis=0).astype(BF16)

    if not bulk_o:
        kap_ref[...] = kap
        kapa_ref[...] = kap * a
        w_ref[...] = jnp.exp(log_w)
    else:
        tri = (lax.broadcasted_iota(jnp.int32, (tc, tc), 0)
               >= lax.broadcasted_iota(jnp.int32, (tc, tc), 1)).astype(BF16)
        cum = jnp.concatenate([sum(jnp.dot(tri, t, preferred_element_type=F32) for t in _split3(log_w[rows(b)]))
                               for b in range(bb)], axis=0)
        inv_gamma = jnp.exp(-cum)
        gamma = jnp.exp(cum)
        rt = zr * gamma
        kt = km * inv_gamma
        kapt = kap * jnp.exp(cum - log_w)
        w_ref[...] = gamma
        kap_ref[...] = kapt
        bt_ref[...] = kap * a * inv_gamma
        rr = lax.broadcasted_iota(jnp.int32, (LANES, LANES), 0) & (NH - 1)
        cc = lax.broadcasted_iota(jnp.int32, (LANES, LANES), 1) & (NH - 1)
        pad_rows = jnp.zeros((LANES - tc, LANES), F32)
        for b in range(bb):
            for c in range(HEAD_PAIRS):
                rhm = head_rows(rt[rows(b), lanes(c)])
                khm = head_rows(kt[rows(b), lanes(c)])
                rhm_ref[b, c] = rhm
                khm_ref[b, c] = khm
                ob_ref[b, c] = _dot_nt(s_ref[b, c].astype(BF16), rhm)
                av_ref[b, c] = jnp.where(rr < cc, _dot_nt(khm, head_rows(kapt[rows(b), lanes(c)])), 0.0).astype(BF16)
                vt = jnp.concatenate([zv[rows(b), lanes(c)], pad_rows], axis=0).T
                vc_ref[b, c] = jnp.where(left, vt[0:NH], pltpu.roll(vt[NH:2 * NH], shift=NH, axis=1)).astype(BF16)
        for b in range(bb):
            for c in range(HEAD_PAIRS):
                g1_ref[b, c] = jnp.dot(vc_ref[b, c], av_ref[b, c], preferred_element_type=F32)

    def bulk_step(t, carry):
        omask = lane_t == t
        for g0 in range(0, bb, gb):
            group = range(g0, g0 + gb)
            lhs, bts = [], []
            for b in group:
                row = pl.ds(b * tc + t, 1)
                kap_t = kap_ref[row, :]
                bts.append(bt_ref[row, :])
                for c0 in range(0, HEAD_PAIRS, 2):
                    ps = []
                    for c in (c0, c0 + 1):
                        p = s_ref[b, c] * kap_t[:, lanes(c)]
                        ps.append(jnp.where(omask, p + g1_ref[b, c], p).astype(BF16))
                    lhs.append(jnp.concatenate(ps, axis=1))
            red = jnp.dot(jnp.concatenate(lhs, axis=0), ones2, preferred_element_type=F32)
            for bi, b in enumerate(group):
                for c in range(HEAD_PAIRS):
                    base = (bi * (HEAD_PAIRS // 2) + c // 2) * NH
                    skk = red[base:base + NH, (c % 2) * LANES:(c % 2 + 1) * LANES]
                    s_ref[b, c] = s_ref[b, c] - skk * bts[bi][:, lanes(c)]
                    pltpu.store(u_ref.at[b, c], skk, mask=omask)
        return carry

    def out_products(t_out, with_update, t):
        omask = lane_t == t_out
        for g0 in range(0, bb, gb):
            group = range(g0, g0 + gb)
            lhs, rowvecs = [], []
            for b in group:
                r_p = r_ref[pl.ds(b * tc + jnp.maximum(t_out, 0), 1), :]
                if with_update:
                    row = pl.ds(b * tc + t, 1)
                    kap_t, v_t = kap_ref[row, :], v_ref[row, :]
                    rowvecs.append((kapa_ref[row, :], w_ref[row, :], km_ref[row, :]))
                    for c0 in range(0, HEAD_PAIRS, 2):
                        dv = []
                        for c in (c0, c0 + 1):
                            s = s_ref[b, c]
                            lhs.append(jnp.concatenate([(s * kap_t[:, lanes(c)]).astype(BF16),
                                                        (s * r_p[:, lanes(c)]).astype(BF16)], axis=1))
                            dv.append(eye16 * jnp.broadcast_to(v_t[:, lanes(c)], (NH, LANES)).astype(BF16))
                        lhs.append(jnp.concatenate(dv, axis=1))
                else:
                    for c in range(0, HEAD_PAIRS, 2):
                        lhs.append(jnp.concatenate([(s_ref[b, c] * r_p[:, lanes(c)]).astype(BF16),
                                                    (s_ref[b, c + 1] * r_p[:, lanes(c + 1)]).astype(BF16)], axis=1))
            red = jnp.dot(jnp.concatenate(lhs, axis=0), ones2, preferred_element_type=F32)
            for bi, b in enumerate(group):
                for c in range(HEAD_PAIRS):
                    half = slice((c % 2) * LANES, (c % 2 + 1) * LANES)
                    if with_update:
                        base = ((bi * (HEAD_PAIRS // 2) + c // 2) * 3 + c % 2) * NH
                        skk = red[base:base + NH, 0:LANES]
                        ob = red[base:base + NH, LANES:2 * LANES]
                        vbase = ((bi * (HEAD_PAIRS // 2) + c // 2) * 3 + 2) * NH
                        vb = red[vbase:vbase + NH, half]
                        kapa_t, w_t, km_t = rowvecs[bi]
                        s_ref[b, c] = (s_ref[b, c] * w_t[:, lanes(c)] - skk * kapa_t[:, lanes(c)]
                                       + vb * km_t[:, lanes(c)])
                    else:
                        base = (bi * (HEAD_PAIRS // 2) + c // 2) * NH
                        ob = red[base:base + NH, half]
                    pltpu.store(ob_ref.at[b, c], ob, mask=omask)

    if bulk_o:
        lax.fori_loop(0, tc, bulk_step, 0, unroll=8)
        for b in range(bb):
            for c in range(HEAD_PAIRS):
                vk = jnp.dot(vc_ref[b, c], khm_ref[b, c], preferred_element_type=F32)
                s_ref[b, c] = (s_ref[b, c] + vk) * w_ref[(b + 1) * tc - 1:(b + 1) * tc, lanes(c)]
    else:
        def step(t, carry):
            out_products(t - 1, True, t)
            return carry

        lax.fori_loop(0, tc, step, 0, unroll=8)
        out_products(tc - 1, False, None)

    for b in range(bb):
        for c in range(HEAD_PAIRS):
            st_ref[b, 2 * c] = s_ref[b, c, :, 0:NH]
            st_ref[b, 2 * c + 1] = s_ref[b, c, :, NH:2 * NH]

    if bulk_o:
        causal = rr <= cc
        for b in range(bb):
            for c in range(HEAD_PAIRS):
                rhm = rhm_ref[b, c]
                av_ref[b, c] = jnp.where(causal, _dot_nt(khm_ref[b, c], rhm), 0.0).astype(BF16)
                au_ref[b, c] = jnp.where(causal, _dot_nt(head_rows(bt_ref[rows(b), lanes(c)]), rhm), 0.0).astype(BF16)
        for b in range(bb):
            for c in range(HEAD_PAIRS):
                ob_ref[b, c] = (ob_ref[b, c]
                                + jnp.dot(vc_ref[b, c], av_ref[b, c], preferred_element_type=F32)
                                - jnp.dot(u_ref[b, c].astype(BF16), au_ref[b, c], preferred_element_type=F32))

    inv_n = 1.0 / RWKV_HEAD
    o_rows = []
    for b in range(bb):
        o_chunks = []
        for c in range(0, HEAD_PAIRS, 2):
            xt = jnp.concatenate([ob_ref[b, c], ob_ref[b, c + 1]], axis=0).T
            top, bot = xt[0:NH], xt[NH:2 * NH]
            o_chunks.append(jnp.where(left, top, pltpu.roll(bot, shift=NH, axis=1))[0:tc])
            o_chunks.append(jnp.where(left, pltpu.roll(top, shift=NH, axis=1), bot)[0:tc])
        o_rows.append(jnp.concatenate(o_chunks, axis=1))
    o = jnp.concatenate(o_rows, axis=0)
    mean = head_sums(o) * inv_n
    d = o - mean
    var = head_sums(d * d) * inv_n
    on = d * lax.rsqrt(var + GN_EPS) * lnw_ref[...] + lnb_ref[...]
    bonus = head_sums(r_ref[...] * km_ref[...] * rk_ref[...]) * v_ref[...]
    y = (on + bonus) * gate_ref[...]
    for b in range(bb):
        y_ref[b] = y[rows(b)]


def _rwkv(p3, shift0, s0, layer, mu, w0, ww, a0, wa, wg, kk, ka, rk, lnw, lnb, bb, tc):
    B, T, _ = p3.shape
    W = RWKV_WIDTH
    SW = 3 * W + LORA_PAD
    row = lambda: pl.BlockSpec((1, W), lambda b, t: (0, 0))
    lora = lambda rows: pl.BlockSpec((rows, W), lambda b, t: (0, 0))
    state = lambda: pl.BlockSpec((bb, RWKV_HEADS, RWKV_HEAD, RWKV_HEAD), lambda b, t: (b, 0, 0, 0))
    state_in = pl.BlockSpec((None, bb, RWKV_HEADS, RWKV_HEAD, RWKV_HEAD), lambda b, t: (layer, b, 0, 0, 0))
    chunk = lambda: pltpu.VMEM((bb * tc, W), F32)
    packed = lambda: pltpu.VMEM((bb, HEAD_PAIRS, RWKV_HEAD, LANES), F32)
    pair_mat = lambda: pltpu.VMEM((bb, HEAD_PAIRS, LANES, LANES), BF16)
    return pl.pallas_call(
        functools.partial(_rwkv_kernel, bb, tc),
        grid=(B // bb, T // tc),
        in_specs=[pl.BlockSpec((bb, tc, W), lambda b, t: (b, t, 2)),
                  pl.BlockSpec((bb, tc, W), lambda b, t: (b, t, 3)),
                  pl.BlockSpec((bb, tc, W), lambda b, t: (b, t, 4)),
                  pl.BlockSpec((bb, tc, LORA_PAD), lambda b, t: (b, t, LORA_COL_BLOCK)),
                  pl.BlockSpec((bb, 1, SW), lambda b, t: (b, 0, 0)),
                  state_in,
                  pl.BlockSpec((1, SW), lambda b, t: (0, 0)),
                  row(), lora(LORA_WA_PAD), row(), lora(LORA_WA_PAD), lora(LORA_G_PAD),
                  row(), row(), row(), row(), row()],
        out_specs=[pl.BlockSpec((bb, tc, W), lambda b, t: (b, t, 0)), state()],
        out_shape=[jax.ShapeDtypeStruct((B, T, W), F32),
                   jax.ShapeDtypeStruct((B, RWKV_HEADS, RWKV_HEAD, RWKV_HEAD), F32)],
        scratch_shapes=[packed(), pltpu.VMEM((bb, 1, SW), F32),
                        chunk(), chunk(), chunk(), chunk(), chunk(), chunk(), chunk(),
                        packed(), packed(), packed(), chunk(), pair_mat(), pair_mat(), pair_mat(), pair_mat(),
                        pltpu.VMEM((bb, HEAD_PAIRS, RWKV_HEAD, LANES), BF16)],
        compiler_params=_cparams(("arbitrary", "arbitrary")),
        name="rwkv7",
    )(p3, p3, p3, p3, shift0, s0, mu, w0, ww, a0, wa, wg, kk, ka, rk, lnw, lnb)


def _pad_shift(a):
    pad = [(0, 0)] * (a.ndim - 1) + [(0, LORA_PAD - LORA_TOTAL)]
    return jnp.pad(a, pad)


def _lora_rows(w_up, offset, rows):
    rank = w_up.shape[0]
    return jnp.pad(w_up, ((offset, rows - offset - rank), (0, 0))).astype(BF16)


def _hybrid_layer(x, mem_k, mem_v, h0, conv0, S0, shift0, wts, layer, cfg):
    B, T, D = x.shape
    M = B * T
    x2 = x.reshape(M, D)

    p = _matmul(x2, wts["w_in_t"], layer, cfg["tm_mm"], LORA_PAD, w_tail=wts["w_in_tail"], w_is_nk=True)
    p3 = p.reshape(B, T, P_PAD)

    y_lru, h_last, conv_new = _lru(p3, h0.reshape(B, 1, LRU_WIDTH), conv0,
                                   wts["conv_w"], wts["conv_b"], wts["lru_wa"], wts["lru_ba"],
                                   wts["lru_wx"], wts["lru_bx"], wts["lru_L"], cfg["lru_bb"], cfg["lru_tc"])

    y_rwkv, s_new = _rwkv(p3, _pad_shift(shift0).reshape(B, 1, -1), S0, layer,
                          wts["mu"], wts["w0"], wts["ww"], wts["a0"], wts["wa"], wts["wg"],
                          wts["k_k"], wts["k_a"], wts["r_k"], wts["ln_w"], wts["ln_b"],
                          cfg["rwkv_bb"], cfg["rwkv_tc"])
    shift_new = p3[:, T - 1, 2 * LRU_WIDTH:P_TOTAL]

    x1 = _matmul_res_ln([y_lru.reshape(M, LRU_WIDTH), y_rwkv.reshape(M, RWKV_WIDTH)],
                        [wts["w_out_a"], wts["w_out_b"]], x2, wts["ln1_g"], wts["ln1_b"], cfg["tm_ln"])

    if mem_k.ndim == 4:
        q = _matmul(x1, wts["wq"], layer, cfg["tm_mm"], 512)
        att = _attention_cache(q.reshape(B, T, D), mem_k, mem_v, cfg["att_bb"])
        x2n = _matmul_res_ln([att.reshape(M, D)], [wts["wo"]], x1, wts["ln2_g"], wts["ln2_b"], cfg["tm_ln"])
    else:
        x2n = _xattn_block(x1.reshape(B, T, D), mem_k, mem_v, wts["wq_b"], wts["wo"],
                           wts["ln2_g"], wts["ln2_b"], cfg["att_tq"]).reshape(M, D)

    x3 = _mlp_ln(x2n, wts["w1"], wts["w2"], wts["ln3_g"], wts["ln3_b"], cfg["tm"], cfg["tf"])
    return (x3.reshape(B, T, D), h_last.reshape(B, LRU_WIDTH), conv_new,
            s_new, shift_new)


def kernel(x_prompt, x_sample, mem_prompt, cache_mem_k, cache_mem_v, state_lru_h, state_lru_conv, state_rwkv_S, state_rwkv_shift, w_in, lru_conv_w, lru_conv_b, lru_wa, lru_ba, lru_wx, lru_bx, lru_L, rwkv_mu, rwkv_w0, rwkv_w_up, rwkv_a0, rwkv_a_up, rwkv_g_up, rwkv_k_k, rwkv_k_a, rwkv_r_k, rwkv_ln_w, rwkv_ln_b, w_out, ln1_g, ln1_b, xa_wq, xa_wk, xa_wv, xa_wo, ln2_g, ln2_b, mlp_w1, mlp_w2, ln3_g, ln3_b):
    B, T, D = x_prompt.shape
    Bs, Ts, _ = x_sample.shape
    assert w_in.shape[0] == DEPTH

    cfg_p = dict(tm_mm=1024, tm=512, tf=1024, tm_ln=256, lru_bb=2, lru_tc=256, rwkv_bb=B, rwkv_tc=64,
                 att_bb=1, att_tq=256)
    cfg_s = dict(tm_mm=1024, tm=512, tf=1024, tm_ln=256, lru_bb=8, lru_tc=Ts, rwkv_bb=8, rwkv_tc=Ts,
                 att_bb=2, att_tq=Ts)

    yp, ys = x_prompt, x_sample
    outs_p = [[] for _ in range(6)]
    outs_s = [[] for _ in range(4)]
    for l in range(DEPTH):
        row = lambda a: a[l].reshape(1, -1)
        wts = dict(
            w_in_t=jnp.swapaxes(w_in, 1, 2),
            w_in_tail=jnp.pad(jnp.swapaxes(w_in, 1, 2)[l, P_PAD - LORA_PAD:],
                              ((0, P_PAD - P_TOTAL), (0, 0))).astype(BF16),
            conv_w=lru_conv_w[l], conv_b=row(lru_conv_b),
            lru_wa=lru_wa[l].astype(BF16), lru_ba=row(lru_ba),
            lru_wx=lru_wx[l].astype(BF16), lru_bx=row(lru_bx), lru_L=row(lru_L),
            mu=_pad_shift(rwkv_mu[l]).reshape(1, -1), w0=row(rwkv_w0), a0=row(rwkv_a0),
            ww=_lora_rows(rwkv_w_up[l], 0, LORA_WA_PAD), wa=_lora_rows(rwkv_a_up[l], DECAY_LORA, LORA_WA_PAD),
            wg=_lora_rows(rwkv_g_up[l], 0, LORA_G_PAD),
            k_k=row(rwkv_k_k), k_a=row(rwkv_k_a), r_k=row(rwkv_r_k), ln_w=row(rwkv_ln_w), ln_b=row(rwkv_ln_b),
            w_out_a=w_out[l, :LRU_WIDTH].astype(BF16), w_out_b=w_out[l, LRU_WIDTH:].astype(BF16),
            ln1_g=row(ln1_g), ln1_b=row(ln1_b),
            wq=xa_wq, wq_b=xa_wq[l].astype(BF16), wo=xa_wo[l].astype(BF16), ln2_g=row(ln2_g), ln2_b=row(ln2_b),
            w1=mlp_w1[l].astype(BF16), w2=mlp_w2[l].astype(BF16), ln3_g=row(ln3_g), ln3_b=row(ln3_b),
        )
        mem2 = mem_prompt.reshape(B * N_MEM, D)
        mem_k = _matmul(mem2, xa_wk, l, 512, 512).reshape(B, N_MEM, D)
        mem_v = _matmul(mem2, xa_wv, l, 512, 512).reshape(B, N_MEM, D)
        yp, hl, cl, Sl, shl = _hybrid_layer(
            yp, mem_k, mem_v,
            jnp.zeros((B, LRU_WIDTH), F32), jnp.zeros((B, CONV_WIDTH - 1, LRU_WIDTH), F32),
            jnp.zeros((1, B, RWKV_HEADS, RWKV_HEAD, RWKV_HEAD), F32), jnp.zeros((B, SHIFT_WIDTH), F32),
            wts, 0, cfg_p)
        for lst, val in zip(outs_p, (mem_k.reshape(B, N_MEM, X_HEADS, X_HEAD_DIM),
                                     mem_v.reshape(B, N_MEM, X_HEADS, X_HEAD_DIM), hl, cl, Sl, shl)):
            lst.append(val)
        ys, hl, cl, Sl, shl = _hybrid_layer(
            ys, cache_mem_k[l], cache_mem_v[l],
            state_lru_h[l], state_lru_conv[l], state_rwkv_S, state_rwkv_shift[l], wts, l, cfg_s)
        for lst, val in zip(outs_s, (hl, cl, Sl, shl)):
            lst.append(val)

    return (yp, ys, *(jnp.stack(o) for o in outs_p), *(jnp.stack(o) for o in outs_s))
```

```python
import functools
import math

import jax
import jax.numpy as jnp
from jax import lax
from jax.experimental import pallas as pl
from jax.experimental.pallas import tpu as pltpu

F32 = jnp.float32
BF16 = jnp.bfloat16

D_MODEL = 2048
LRU_WIDTH = 1024
LRU_BLOCKS = 8
LRU_BLOCK = 128
CONV_WIDTH = 4
LRU_C = 8.0
RWKV_WIDTH = 1024
RWKV_HEAD = 64
RWKV_HEADS = 16
DECAY_LORA = 64
A_LORA = 64
G_LORA = 160
LORA_TOTAL = DECAY_LORA + A_LORA + G_LORA
SHIFT_WIDTH = 3 * RWKV_WIDTH + LORA_TOTAL
P_TOTAL = 2 * LRU_WIDTH + SHIFT_WIDTH
DECAY_SCALE = math.exp(-0.5)
N_MEM = 256
X_HEADS = 4
X_HEAD_DIM = 512
D_FF = 4 * D_MODEL
LN_EPS = 1e-5
GN_EPS = 64e-5
DEPTH = 1
ALPHA = (2 * DEPTH) ** 0.25

LANES = 128
SUBLANES = 8
VMEM_LIMIT_BYTES = 56 * 1024 * 1024

LORA_PAD = 512
LORA_WA_PAD = DECAY_LORA + A_LORA
LORA_G_PAD = 2 * LANES
P_PAD = 5 * 1024 + LORA_PAD
LORA_COL_BLOCK = (5 * 1024) // LORA_PAD
HEAD_PAIRS = RWKV_WIDTH // LANES


def _cparams(sem):
    return pltpu.CompilerParams(dimension_semantics=sem, vmem_limit_bytes=VMEM_LIMIT_BYTES)


def _mm_kernel(n_main, has_tail, w_is_nk, x_ref, w_ref, *rest):
    if has_tail:
        wt_ref, o_ref, xb_ref = rest
    else:
        o_ref, xb_ref = rest
    j = pl.program_id(1)
    contract = (((1,), (1,)), ((), ())) if w_is_nk else (((1,), (0,)), ((), ()))

    @pl.when(j == 0)
    def _():
        xb_ref[...] = x_ref[...].astype(BF16)

    @pl.when(j < n_main)
    def _():
        o_ref[...] = lax.dot_general(xb_ref[...], w_ref[...].astype(BF16), contract, preferred_element_type=F32)

    if has_tail:
        @pl.when(j == n_main)
        def _():
            o_ref[...] = lax.dot_general(xb_ref[...], wt_ref[...], contract, preferred_element_type=F32)


def _matmul(x, w, layer, tm, tn, w_tail=None, w_is_nk=False):
    M, K = x.shape
    N = w.shape[1] if w_is_nk else w.shape[2]
    n_main = N // tn
    has_tail = w_tail is not None
    n_tiles = n_main + int(has_tail)
    assert M % tm == 0 and (has_tail or N % tn == 0)
    if w_is_nk:
        w_spec = pl.BlockSpec((None, tn, K), lambda i, j: (layer, jnp.minimum(j, n_main - 1), 0))
    else:
        w_spec = pl.BlockSpec((None, K, tn), lambda i, j: (layer, 0, jnp.minimum(j, n_main - 1)))
    in_specs = [pl.BlockSpec((tm, K), lambda i, j: (i, 0)), w_spec]
    args = [x, w]
    if has_tail:
        in_specs.append(pl.BlockSpec(w_tail.shape, lambda i, j: (0, 0)))
        args.append(w_tail)
    return pl.pallas_call(
        functools.partial(_mm_kernel, n_main, has_tail, w_is_nk),
        grid=(M // tm, n_tiles),
        in_specs=in_specs,
        out_specs=pl.BlockSpec((tm, tn), lambda i, j: (i, j)),
        out_shape=jax.ShapeDtypeStruct((M, n_tiles * tn), F32),
        scratch_shapes=[pltpu.VMEM((tm, K), BF16)],
        compiler_params=_cparams(("arbitrary", "arbitrary")),
        name="matmul",
    )(*args)


def _layer_norm(y, g, b):
    mu = jnp.mean(y, axis=-1, keepdims=True)
    d = y - mu
    var = jnp.mean(d * d, axis=-1, keepdims=True)
    return d * lax.rsqrt(var + LN_EPS) * g + b


def _mm_ln_kernel(n_in, *refs):
    x_refs = refs[:n_in]
    w_refs = refs[n_in:2 * n_in]
    res_ref, g_ref, b_ref, o_ref = refs[2 * n_in:]
    acc = jnp.dot(x_refs[0][...].astype(BF16), w_refs[0][...], preferred_element_type=F32)
    for x_ref, w_ref in zip(x_refs[1:], w_refs[1:]):
        acc = acc + jnp.dot(x_ref[...].astype(BF16), w_ref[...], preferred_element_type=F32)
    y = ALPHA * res_ref[...] + acc
    o_ref[...] = _layer_norm(y, g_ref[...], b_ref[...])


def _matmul_res_ln(xs, ws, res, g, b, tm):
    M, N = res.shape
    n_in = len(xs)
    in_specs = [pl.BlockSpec((tm, x.shape[1]), lambda i: (i, 0)) for x in xs]
    in_specs += [pl.BlockSpec(w.shape, lambda i: (0, 0)) for w in ws]
    in_specs += [pl.BlockSpec((tm, N), lambda i: (i, 0)),
                 pl.BlockSpec((1, N), lambda i: (0, 0)),
                 pl.BlockSpec((1, N), lambda i: (0, 0))]
    return pl.pallas_call(
        functools.partial(_mm_ln_kernel, n_in),
        grid=(M // tm,),
        in_specs=in_specs,
        out_specs=pl.BlockSpec((tm, N), lambda i: (i, 0)),
        out_shape=jax.ShapeDtypeStruct((M, N), F32),
        compiler_params=_cparams(("arbitrary",)),
        name="matmul_res_ln",
    )(*xs, *ws, res, g, b)


def _mlp_kernel(x_ref, w1_ref, w2_ref, g_ref, b_ref, o_ref, xb_ref, acc_ref):
    f = pl.program_id(1)

    @pl.when(f == 0)
    def _():
        xb_ref[...] = x_ref[...].astype(BF16)
        acc_ref[...] = jnp.zeros_like(acc_ref)

    h = jnp.dot(xb_ref[...], w1_ref[...], preferred_element_type=F32)
    h = jnp.square(jnp.maximum(h, 0.0))
    acc_ref[...] += jnp.dot(h.astype(BF16), w2_ref[...], preferred_element_type=F32)

    @pl.when(f == pl.num_programs(1) - 1)
    def _():
        y = ALPHA * x_ref[...] + acc_ref[...]
        o_ref[...] = _layer_norm(y, g_ref[...], b_ref[...])


def _mlp_ln(x, w1, w2, g, b, tm, tf):
    M, D = x.shape
    FF = w1.shape[1]
    return pl.pallas_call(
        _mlp_kernel,
        grid=(M // tm, FF // tf),
        in_specs=[pl.BlockSpec((tm, D), lambda i, f: (i, 0)),
                  pl.BlockSpec((D, tf), lambda i, f: (0, f)),
                  pl.BlockSpec((tf, D), lambda i, f: (f, 0)),
                  pl.BlockSpec((1, D), lambda i, f: (0, 0)),
                  pl.BlockSpec((1, D), lambda i, f: (0, 0))],
        out_specs=pl.BlockSpec((tm, D), lambda i, f: (i, 0)),
        out_shape=jax.ShapeDtypeStruct((M, D), F32),
        scratch_shapes=[pltpu.VMEM((tm, D), BF16), pltpu.VMEM((tm, D), F32)],
        compiler_params=_cparams(("arbitrary", "arbitrary")),
        name="mlp_ln",
    )(x, w1, w2, g, b)


def _softmax_pv(s, v):
    m = jnp.max(s, axis=-1, keepdims=True)
    e = jnp.exp(s - m)
    p = e / jnp.sum(e, axis=-1, keepdims=True)
    return jnp.dot(p.astype(BF16), v, preferred_element_type=F32)


def _attn_cache_kernel(bb, q_ref, k_ref, v_ref, o_ref):
    scale = X_HEAD_DIM ** -0.5
    T = q_ref.shape[1]
    rows = N_MEM * X_HEADS
    q_head = lax.broadcasted_iota(jnp.int32, (X_HEADS * T, rows), 0) // T
    k_head = lax.broadcasted_iota(jnp.int32, (X_HEADS * T, rows), 1) & (X_HEADS - 1)
    own_head = q_head == k_head
    for b in range(bb):
        q = jnp.concatenate([q_ref[b, :, h * X_HEAD_DIM:(h + 1) * X_HEAD_DIM] for h in range(X_HEADS)], axis=0)
        k = k_ref[b].reshape(rows, X_HEAD_DIM).astype(BF16)
        v = v_ref[b].reshape(rows, X_HEAD_DIM).astype(BF16)
        s = lax.dot_general(q.astype(BF16), k, (((1,), (1,)), ((), ())), preferred_element_type=F32) * scale
        o = _softmax_pv(jnp.where(own_head, s, -jnp.inf), v)
        for h in range(X_HEADS):
            o_ref[b, :, h * X_HEAD_DIM:(h + 1) * X_HEAD_DIM] = o[h * T:(h + 1) * T]


def _attention_cache(q, cache_k, cache_v, bb):
    B, T, D = q.shape
    kv_spec = lambda: pl.BlockSpec((bb, N_MEM, X_HEADS, X_HEAD_DIM), lambda b: (b, 0, 0, 0))
    return pl.pallas_call(
        functools.partial(_attn_cache_kernel, bb),
        grid=(B // bb,),
        in_specs=[pl.BlockSpec((bb, T, D), lambda b: (b, 0, 0)), kv_spec(), kv_spec()],
        out_specs=pl.BlockSpec((bb, T, D), lambda b: (b, 0, 0)),
        out_shape=jax.ShapeDtypeStruct((B, T, D), F32),
        compiler_params=_cparams(("arbitrary",)),
        name="cache_attention",
    )(q, cache_k, cache_v)


def _xattn_block_kernel(x_ref, k_ref, v_ref, wq_ref, wo_ref, g_ref, b_ref, o_ref, kb_ref, vb_ref):
    @pl.when(pl.program_id(1) == 0)
    def _():
        kb_ref[...] = k_ref[0].astype(BF16)
        vb_ref[...] = v_ref[0].astype(BF16)

    scale = X_HEAD_DIM ** -0.5
    x = x_ref[0]
    q = jnp.dot(x.astype(BF16), wq_ref[...], preferred_element_type=F32)
    heads = []
    for h in range(X_HEADS):
        cols = slice(h * X_HEAD_DIM, (h + 1) * X_HEAD_DIM)
        s = lax.dot_general(q[:, cols].astype(BF16), kb_ref[:, cols], (((1,), (1,)), ((), ())),
                            preferred_element_type=F32) * scale
        heads.append(_softmax_pv(s, vb_ref[:, cols]).astype(BF16))
    att = jnp.concatenate(heads, axis=1)
    y = ALPHA * x + jnp.dot(att, wo_ref[...], preferred_element_type=F32)
    o_ref[0] = _layer_norm(y, g_ref[...], b_ref[...])


def _xattn_block(x, mem_k, mem_v, wq, wo, g, b, tq):
    B, T, D = x.shape
    resident = lambda: pl.BlockSpec((D, D), lambda i, t: (0, 0), pipeline_mode=pl.Buffered(1))
    kv = lambda: pl.BlockSpec((1, N_MEM, D), lambda i, t: (i, 0, 0))
    row = lambda: pl.BlockSpec((1, D), lambda i, t: (0, 0))
    return pl.pallas_call(
        _xattn_block_kernel,
        grid=(B, T // tq),
        in_specs=[pl.BlockSpec((1, tq, D), lambda i, t: (i, t, 0)), kv(), kv(), resident(), resident(), row(), row()],
        out_specs=pl.BlockSpec((1, tq, D), lambda i, t: (i, t, 0)),
        out_shape=jax.ShapeDtypeStruct((B, T, D), F32),
        scratch_shapes=[pltpu.VMEM((N_MEM, D), BF16), pltpu.VMEM((N_MEM, D), BF16)],
        compiler_params=_cparams(("arbitrary", "arbitrary")),
        name="xattn_block",
    )(x, mem_k, mem_v, wq, wo, g, b)


def _softplus(x):
    return jnp.maximum(x, 0.0) + jnp.log1p(jnp.exp(-jnp.abs(x)))


def _lru_kernel(bb, tc, ux_ref, ug_ref, h0_ref, c0_ref, cw_ref, cb_ref, wa_ref, ba_ref, wx_ref, bx_ref,
                lam_ref, y_ref, hl_ref, cn_ref, ext_ref, hc_ref, a_ref, b_ref, hs_ref):
    ti = pl.program_id(1)
    pad = SUBLANES
    neg_c_sp = -LRU_C * _softplus(-lam_ref[...])

    for b in range(bb):
        @pl.when(ti == 0)
        def _():
            ext_ref[b, pad - 3:pad, :] = c0_ref[b]
            hc_ref[b] = h0_ref[b]

        ext_ref[b, pad:pad + tc, :] = ux_ref[b]
        xc = cb_ref[...] + cw_ref[0:1, :] * ext_ref[b, pad - 3:pad - 3 + tc, :]
        for j in range(1, CONV_WIDTH):
            xc = xc + cw_ref[j:j + 1, :] * ext_ref[b, pad - 3 + j:pad - 3 + j + tc, :]

        ga, gx = [], []
        for n in range(LRU_BLOCKS):
            xb = xc[:, n * LRU_BLOCK:(n + 1) * LRU_BLOCK].astype(BF16)
            ga.append(jnp.dot(xb, wa_ref[n], preferred_element_type=F32))
            gx.append(jnp.dot(xb, wx_ref[n], preferred_element_type=F32))
        r = jax.nn.sigmoid(jnp.concatenate(ga, axis=1) + ba_ref[...])
        i = jax.nn.sigmoid(jnp.concatenate(gx, axis=1) + bx_ref[...])
        log_a = neg_c_sp * r
        rs = slice(b * tc, (b + 1) * tc)
        a_ref[rs, :] = jnp.exp(log_a)
        b_ref[rs, :] = jnp.sqrt(1.0 - jnp.exp(2.0 * log_a)) * (i * xc)
        tail = ext_ref[b, pad + tc - 3:pad + tc, :]
        cn_ref[b] = tail
        ext_ref[b, pad - 3:pad, :] = tail

    def step(t, hs):
        out = []
        for b in range(bb):
            row = pl.ds(b * tc + t, 1)
            h = a_ref[row, :] * hs[b] + b_ref[row, :]
            hs_ref[row, :] = h
            out.append(h)
        return tuple(out)

    hs = lax.fori_loop(0, tc, step, tuple(hc_ref[b] for b in range(bb)))
    for b in range(bb):
        hc_ref[b] = hs[b]
        hl_ref[b] = hs[b]
        y_ref[b] = hs_ref[b * tc:(b + 1) * tc, :] * jax.nn.gelu(ug_ref[b])


def _lru(p3, h0, conv0, cw, cb, wa, ba, wx, bx, lam, bb, tc):
    B, T, _ = p3.shape
    W = LRU_WIDTH
    row = lambda: pl.BlockSpec((1, W), lambda b, t: (0, 0))
    blk = lambda: pl.BlockSpec((LRU_BLOCKS, LRU_BLOCK, LRU_BLOCK), lambda b, t: (0, 0, 0))
    return pl.pallas_call(
        functools.partial(_lru_kernel, bb, tc),
        grid=(B // bb, T // tc),
        in_specs=[pl.BlockSpec((bb, tc, W), lambda b, t: (b, t, 0)),
                  pl.BlockSpec((bb, tc, W), lambda b, t: (b, t, 1)),
                  pl.BlockSpec((bb, 1, W), lambda b, t: (b, 0, 0)),
                  pl.BlockSpec((bb, CONV_WIDTH - 1, W), lambda b, t: (b, 0, 0)),
                  pl.BlockSpec((CONV_WIDTH, W), lambda b, t: (0, 0)),
                  row(), blk(), row(), blk(), row(), row()],
        out_specs=[pl.BlockSpec((bb, tc, W), lambda b, t: (b, t, 0)),
                   pl.BlockSpec((bb, 1, W), lambda b, t: (b, 0, 0)),
                   pl.BlockSpec((bb, CONV_WIDTH - 1, W), lambda b, t: (b, 0, 0))],
        out_shape=[jax.ShapeDtypeStruct((B, T, W), F32),
                   jax.ShapeDtypeStruct((B, 1, W), F32),
                   jax.ShapeDtypeStruct((B, CONV_WIDTH - 1, W), F32)],
        scratch_shapes=[pltpu.VMEM((bb, tc + SUBLANES, W), F32),
                        pltpu.VMEM((bb, 1, W), F32),
                        pltpu.VMEM((bb * tc, W), F32),
                        pltpu.VMEM((bb * tc, W), F32),
                        pltpu.VMEM((bb * tc, W), F32)],
        compiler_params=_cparams(("arbitrary", "arbitrary")),
        name="rg_lru",
    )(p3, p3, h0, conv0, cw, cb, wa, ba, wx, bx, lam)


def _head_ones(n):
    ri = lax.broadcasted_iota(jnp.int32, (n, n), 0)
    ci = lax.broadcasted_iota(jnp.int32, (n, n), 1)
    return ((ri >> 6) == (ci >> 6)).astype(BF16)


def _split2(x):
    hi = x.astype(BF16)
    return hi, (x - hi.astype(F32)).astype(BF16)


def _head_sum(x, ones):
    return sum(jnp.dot(t, ones, preferred_element_type=F32) for t in _split2(x))


def _dot_nt(a, b):
    return lax.dot_general(a, b, (((1,), (1,)), ((), ())), preferred_element_type=F32)


def _rwkv_kernel(bb, tc, ur_ref, uk_ref, uv_ref, ul_ref, sh0_ref, s0_ref, mu_ref, w0_ref, ww_ref, a0_ref,
                 wa_ref, wg_ref, kk_ref, ka_ref, rk_ref, lnw_ref, lnb_ref,
                 y_ref, st_ref,
                 s_ref, carry_ref, kap_ref, kapa_ref, w_ref, km_ref, v_ref, r_ref, gate_ref, ob_ref,
                 u_ref, g1_ref, bt_ref, rhm_ref, khm_ref, av_ref, au_ref, vc_ref):
    ti = pl.program_id(1)
    ones = _head_ones(LANES)
    ones2 = _head_ones(2 * LANES)
    W = RWKV_WIDTH
    NH = RWKV_HEAD
    bulk_o = tc == NH
    row0 = lax.broadcasted_iota(jnp.int32, (tc, 1), 0) == 0
    sub_i = lax.broadcasted_iota(jnp.int32, (NH, LANES), 0)
    lane_i = lax.broadcasted_iota(jnp.int32, (NH, LANES), 1)
    lane_t = lane_i & (NH - 1)
    left = lane_i < NH
    eye16 = (lane_t == sub_i).astype(BF16)
    gb = 2 if bb % 2 == 0 else 1

    def lanes(c):
        return slice(c * LANES, (c + 1) * LANES)

    def rows(b):
        return slice(b * tc, (b + 1) * tc)

    @pl.when(ti == 0)
    def _():
        for b in range(bb):
            for c in range(HEAD_PAIRS):
                s_ref[b, c, :, 0:NH] = s0_ref[b, 2 * c]
                s_ref[b, c, :, NH:2 * NH] = s0_ref[b, 2 * c + 1]
        carry_ref[...] = sh0_ref[...]
        ob_ref[...] = jnp.zeros_like(ob_ref)

    def shifted(u, b, lo):
        cols = slice(lo, lo + u.shape[1])
        prev = jnp.where(row0, carry_ref[b, :, cols], pltpu.roll(u, shift=1, axis=0))
        return u + (prev - u) * mu_ref[:, cols]

    def head_sums(x):
        return jnp.concatenate([_head_sum(x[:, lanes(c)], ones) for c in range(HEAD_PAIRS)], axis=1)

    zs = [[], [], [], []]
    for b in range(bb):
        for z, src, lo in zip(zs, (ur_ref, uk_ref, uv_ref, ul_ref), (0, W, 2 * W, 3 * W)):
            z.append(shifted(src[b], b, lo))
    for b in range(bb):
        for src, lo in zip((ur_ref, uk_ref, uv_ref, ul_ref), (0, W, 2 * W, 3 * W)):
            carry_ref[b, :, lo:lo + src.shape[2]] = src[b, tc - 1:tc, :]
    zr, zk, zv, zl = (jnp.concatenate(z, axis=0) for z in zs)

    z_wa = zl[:, 0:LORA_WA_PAD]
    z_g = zl[:, LORA_WA_PAD:LORA_WA_PAD + LORA_G_PAD]
    dw = jnp.dot(jnp.tanh(z_wa).astype(BF16), ww_ref[...], preferred_element_type=F32)
    da = jnp.dot(z_wa.astype(BF16), wa_ref[...], preferred_element_type=F32)
    gate_ref[...] = jnp.dot(jax.nn.sigmoid(z_g).astype(BF16), wg_ref[...], preferred_element_type=F32)
    log_w = -DECAY_SCALE * jax.nn.sigmoid(w0_ref[...] + dw)
    a = jax.nn.sigmoid(a0_ref[...] + da)
    kk = zk * kk_ref[...]
    kap = kk * lax.rsqrt(jnp.maximum(head_sums(kk * kk), 1e-24))
    km = zk * (1.0 + (a - 1.0) * ka_ref[...])
    km_ref[...] = km
    v_ref[...] = zv
    r_ref[...] = zr

    def head_rows(x):
        return jnp.concatenate([jnp.where(left, x, 0.0), jnp.where(left, 0.0, x)], axis=0).astype(BF16)

    if not bulk_o:
        kap_ref[...] = kap
        kapa_ref[...] = kap * a
        w_ref[...] = jnp.exp(log_w)
    else:
        tri = (lax.broadcasted_iota(jnp.int32, (tc, tc), 0)
               >= lax.broadcasted_iota(jnp.int32, (tc, tc), 1)).astype(BF16)
        cum = jnp.concatenate([sum(jnp.dot(tri, t, preferred_element_type=F32) for t in _split2(log_w[rows(b)]))
                               for b in range(bb)], axis=0)
        inv_gamma = jnp.exp(-cum)
        gamma = jnp.exp(cum)
        rt = zr * gamma
        kt = km * inv_gamma
        kapt = kap * jnp.exp(cum - log_w)
        w_ref[...] = gamma
        kap_ref[...] = kapt
        bt_ref[...] = kap * a * inv_gamma
        rr = lax.broadcasted_iota(jnp.int32, (LANES, LANES), 0) & (NH - 1)
        cc = lax.broadcasted_iota(jnp.int32, (LANES, LANES), 1) & (NH - 1)
        pad_rows = jnp.zeros((LANES - tc, LANES), F32)
        for b in range(bb):
            for c in range(HEAD_PAIRS):
                rhm = head_rows(rt[rows(b), lanes(c)])
                khm = head_rows(kt[rows(b), lanes(c)])
                rhm_ref[b, c] = rhm
                khm_ref[b, c] = khm
                ob_ref[b, c] = _dot_nt(s_ref[b, c].astype(BF16), rhm)
                av_ref[b, c] = jnp.where(rr < cc, _dot_nt(khm, head_rows(kapt[rows(b), lanes(c)])), 0.0).astype(BF16)
                vt = jnp.concatenate([zv[rows(b), lanes(c)], pad_rows], axis=0).T
                vc_ref[b, c] = jnp.where(left, vt[0:NH], pltpu.roll(vt[NH:2 * NH], shift=NH, axis=1)).astype(BF16)
        for b in range(bb):
            for c in range(HEAD_PAIRS):
                g1_ref[b, c] = jnp.dot(vc_ref[b, c], av_ref[b, c], preferred_element_type=F32)

    def bulk_step(t, carry):
        omask = lane_t == t
        for g0 in range(0, bb, gb):
            group = range(g0, g0 + gb)
            lhs, bts = [], []
            for b in group:
                row = pl.ds(b * tc + t, 1)
                kap_t = kap_ref[row, :]
                bts.append(bt_ref[row, :])
                for c0 in range(0, HEAD_PAIRS, 2):
                    ps = []
                    for c in (c0, c0 + 1):
                        p = s_ref[b, c] * kap_t[:, lanes(c)]
                        ps.append(jnp.where(omask, p + g1_ref[b, c], p).astype(BF16))
                    lhs.append(jnp.concatenate(ps, axis=1))
            red = jnp.dot(jnp.concatenate(lhs, axis=0), ones2, preferred_element_type=F32)
            for bi, b in enumerate(group):
                for c in range(HEAD_PAIRS):
                    base = (bi * (HEAD_PAIRS // 2) + c // 2) * NH
                    skk = red[base:base + NH, (c % 2) * LANES:(c % 2 + 1) * LANES]
                    s_ref[b, c] = s_ref[b, c] - skk * bts[bi][:, lanes(c)]
                    pltpu.store(u_ref.at[b, c], skk, mask=omask)
        return carry

    def out_products(t_out, with_update, t):
        omask = lane_t == t_out
        for g0 in range(0, bb, gb):
            group = range(g0, g0 + gb)
            lhs, rowvecs = [], []
            for b in group:
                r_p = r_ref[pl.ds(b * tc + jnp.maximum(t_out, 0), 1), :]
                if with_update:
                    row = pl.ds(b * tc + t, 1)
                    kap_t, v_t = kap_ref[row, :], v_ref[row, :]
                    rowvecs.append((kapa_ref[row, :], w_ref[row, :], km_ref[row, :]))
                    for c0 in range(0, HEAD_PAIRS, 2):
                        dv = []
                        for c in (c0, c0 + 1):
                            s = s_ref[b, c]
                            lhs.append(jnp.concatenate([(s * kap_t[:, lanes(c)]).astype(BF16),
                                                        (s * r_p[:, lanes(c)]).astype(BF16)], axis=1))
                            dv.append(eye16 * jnp.broadcast_to(v_t[:, lanes(c)], (NH, LANES)).astype(BF16))
                        lhs.append(jnp.concatenate(dv, axis=1))
                else:
                    for c in range(0, HEAD_PAIRS, 2):
                        lhs.append(jnp.concatenate([(s_ref[b, c] * r_p[:, lanes(c)]).astype(BF16),
                                                    (s_ref[b, c + 1] * r_p[:, lanes(c + 1)]).astype(BF16)], axis=1))
            red = jnp.dot(jnp.concatenate(lhs, axis=0), ones2, preferred_element_type=F32)
            for bi, b in enumerate(group):
                for c in range(HEAD_PAIRS):
                    half = slice((c % 2) * LANES, (c % 2 + 1) * LANES)
                    if with_update:
                        base = ((bi * (HEAD_PAIRS // 2) + c // 2) * 3 + c % 2) * NH
                        skk = red[base:base + NH, 0:LANES]
                        ob = red[base:base + NH, LANES:2 * LANES]
                        vbase = ((bi * (HEAD_PAIRS // 2) + c // 2) * 3 + 2) * NH
                        vb = red[vbase:vbase + NH, half]
                        kapa_t, w_t, km_t = rowvecs[bi]
                        s_ref[b, c] = (s_ref[b, c] * w_t[:, lanes(c)] - skk * kapa_t[:, lanes(c)]
                                       + vb * km_t[:, lanes(c)])
                    else:
                        base = (bi * (HEAD_PAIRS // 2) + c // 2) * NH
                        ob = red[base:base + NH, half]
                    pltpu.store(ob_ref.at[b, c], ob, mask=omask)

    if bulk_o:
        lax.fori_loop(0, tc, bulk_step, 0, unroll=8)
        for b in range(bb):
            for c in range(HEAD_PAIRS):
                vk = jnp.dot(vc_ref[b, c], khm_ref[b, c], preferred_element_type=F32)
                s_ref[b, c] = (s_ref[b, c] + vk) * w_ref[(b + 1) * tc - 1:(b + 1) * tc, lanes(c)]
    else:
        def step(t, carry):
            out_products(t - 1, True, t)
            return carry

        lax.fori_loop(0, tc, step, 0, unroll=8)
        out_products(tc - 1, False, None)

    for b in range(bb):
        for c in range(HEAD_PAIRS):
            st_ref[b, 2 * c] = s_ref[b, c, :, 0:NH]
            st_ref[b, 2 * c + 1] = s_ref[b, c, :, NH:2 * NH]

    if bulk_o:
        causal = rr <= cc
        for b in range(bb):
            for c in range(HEAD_PAIRS):
                rhm = rhm_ref[b, c]
                av_ref[b, c] = jnp.where(causal, _dot_nt(khm_ref[b, c], rhm), 0.0).astype(BF16)
                au_ref[b, c] = jnp.where(causal, _dot_nt(head_rows(bt_ref[rows(b), lanes(c)]), rhm), 0.0).astype(BF16)
        for b in range(bb):
            for c in range(HEAD_PAIRS):
                ob_ref[b, c] = (ob_ref[b, c]
                                + jnp.dot(vc_ref[b, c], av_ref[b, c], preferred_element_type=F32)
                                - jnp.dot(u_ref[b, c].astype(BF16), au_ref[b, c], preferred_element_type=F32))

    inv_n = 1.0 / RWKV_HEAD
    o_rows = []
    for b in range(bb):
        o_chunks = []
        for c in range(0, HEAD_PAIRS, 2):
            xt = jnp.concatenate([ob_ref[b, c], ob_ref[b, c + 1]], axis=0).T
            top, bot = xt[0:NH], xt[NH:2 * NH]
            o_chunks.append(jnp.where(left, top, pltpu.roll(bot, shift=NH, axis=1))[0:tc])
            o_chunks.append(jnp.where(left, pltpu.roll(top, shift=NH, axis=1), bot)[0:tc])
        o_rows.append(jnp.concatenate(o_chunks, axis=1))
    o = jnp.concatenate(o_rows, axis=0)
    mean = head_sums(o) * inv_n
    d = o - mean
    var = head_sums(d * d) * inv_n
    on = d * lax.rsqrt(var + GN_EPS) * lnw_ref[...] + lnb_ref[...]
    bonus = head_sums(r_ref[...] * km_ref[...] * rk_ref[...]) * v_ref[...]
    y = (on + bonus) * gate_ref[...]
    for b in range(bb):
        y_ref[b] = y[rows(b)]


def _rwkv(p3, shift0, s0, layer, mu, w0, ww, a0, wa, wg, kk, ka, rk, lnw, lnb, bb, tc):
    B, T, _ = p3.shape
    W = RWKV_WIDTH
    SW = 3 * W + LORA_PAD
    row = lambda: pl.BlockSpec((1, W), lambda b, t: (0, 0))
    lora = lambda rows: pl.BlockSpec((rows, W), lambda b, t: (0, 0))
    state = lambda: pl.BlockSpec((bb, RWKV_HEADS, RWKV_HEAD, RWKV_HEAD), lambda b, t: (b, 0, 0, 0))
    state_in = pl.BlockSpec((None, bb, RWKV_HEADS, RWKV_HEAD, RWKV_HEAD), lambda b, t: (layer, b, 0, 0, 0))
    chunk = lambda: pltpu.VMEM((bb * tc, W), F32)
    packed = lambda: pltpu.VMEM((bb, HEAD_PAIRS, RWKV_HEAD, LANES), F32)
    pair_mat = lambda: pltpu.VMEM((bb, HEAD_PAIRS, LANES, LANES), BF16)
    return pl.pallas_call(
        functools.partial(_rwkv_kernel, bb, tc),
        grid=(B // bb, T // tc),
        in_specs=[pl.BlockSpec((bb, tc, W), lambda b, t: (b, t, 2)),
                  pl.BlockSpec((bb, tc, W), lambda b, t: (b, t, 3)),
                  pl.BlockSpec((bb, tc, W), lambda b, t: (b, t, 4)),
                  pl.BlockSpec((bb, tc, LORA_PAD), lambda b, t: (b, t, LORA_COL_BLOCK)),
                  pl.BlockSpec((bb, 1, SW), lambda b, t: (b, 0, 0)),
                  state_in,
                  pl.BlockSpec((1, SW), lambda b, t: (0, 0)),
                  row(), lora(LORA_WA_PAD), row(), lora(LORA_WA_PAD), lora(LORA_G_PAD),
                  row(), row(), row(), row(), row()],
        out_specs=[pl.BlockSpec((bb, tc, W), lambda b, t: (b, t, 0)), state()],
        out_shape=[jax.ShapeDtypeStruct((B, T, W), F32),
                   jax.ShapeDtypeStruct((B, RWKV_HEADS, RWKV_HEAD, RWKV_HEAD), F32)],
        scratch_shapes=[packed(), pltpu.VMEM((bb, 1, SW), F32),
                        chunk(), chunk(), chunk(), chunk(), chunk(), chunk(), chunk(),
                        packed(), packed(), packed(), chunk(), pair_mat(), pair_mat(), pair_mat(), pair_mat(),
                        pltpu.VMEM((bb, HEAD_PAIRS, RWKV_HEAD, LANES), BF16)],
        compiler_params=_cparams(("arbitrary", "arbitrary")),
        name="rwkv7",
    )(p3, p3, p3, p3, shift0, s0, mu, w0, ww, a0, wa, wg, kk, ka, rk, lnw, lnb)


def _pad_shift(a):
    pad = [(0, 0)] * (a.ndim - 1) + [(0, LORA_PAD - LORA_TOTAL)]
    return jnp.pad(a, pad)


def _lora_rows(w_up, offset, rows):
    rank = w_up.shape[0]
    return jnp.pad(w_up, ((offset, rows - offset - rank), (0, 0))).astype(BF16)


def _hybrid_layer(x, mem_k, mem_v, h0, conv0, S0, shift0, wts, layer, cfg):
    B, T, D = x.shape
    M = B * T
    x2 = x.reshape(M, D)

    p = _matmul(x2, wts["w_in_t"], layer, cfg["tm_mm"], LORA_PAD, w_tail=wts["w_in_tail"], w_is_nk=True)
    p3 = p.reshape(B, T, P_PAD)

    y_lru, h_last, conv_new = _lru(p3, h0.reshape(B, 1, LRU_WIDTH), conv0,
                                   wts["conv_w"], wts["conv_b"], wts["lru_wa"], wts["lru_ba"],
                                   wts["lru_wx"], wts["lru_bx"], wts["lru_L"], cfg["lru_bb"], cfg["lru_tc"])

    y_rwkv, s_new = _rwkv(p3, _pad_shift(shift0).reshape(B, 1, -1), S0, layer,
                          wts["mu"], wts["w0"], wts["ww"], wts["a0"], wts["wa"], wts["wg"],
                          wts["k_k"], wts["k_a"], wts["r_k"], wts["ln_w"], wts["ln_b"],
                          cfg["rwkv_bb"], cfg["rwkv_tc"])
    shift_new = p3[:, T - 1, 2 * LRU_WIDTH:P_TOTAL]

    x1 = _matmul_res_ln([y_lru.reshape(M, LRU_WIDTH), y_rwkv.reshape(M, RWKV_WIDTH)],
                        [wts["w_out_a"], wts["w_out_b"]], x2, wts["ln1_g"], wts["ln1_b"], cfg["tm_ln"])

    if mem_k.ndim == 4:
        q = _matmul(x1, wts["wq"], layer, cfg["tm_mm"], 512)
        att = _attention_cache(q.reshape(B, T, D), mem_k, mem_v, cfg["att_bb"])
        x2n = _matmul_res_ln([att.reshape(M, D)], [wts["wo"]], x1, wts["ln2_g"], wts["ln2_b"], cfg["tm_ln"])
    else:
        x2n = _xattn_block(x1.reshape(B, T, D), mem_k, mem_v, wts["wq_b"], wts["wo"],
                           wts["ln2_g"], wts["ln2_b"], cfg["att_tq"]).reshape(M, D)

    x3 = _mlp_ln(x2n, wts["w1"], wts["w2"], wts["ln3_g"], wts["ln3_b"], cfg["tm"], cfg["tf"])
    return (x3.reshape(B, T, D), h_last.reshape(B, LRU_WIDTH), conv_new,
            s_new, shift_new)


def kernel(x_prompt, x_sample, mem_prompt, cache_mem_k, cache_mem_v, state_lru_h, state_lru_conv, state_rwkv_S, state_rwkv_shift, w_in, lru_conv_w, lru_conv_b, lru_wa, lru_ba, lru_wx, lru_bx, lru_L, rwkv_mu, rwkv_w0, rwkv_w_up, rwkv_a0, rwkv_a_up, rwkv_g_up, rwkv_k_k, rwkv_k_a, rwkv_r_k, rwkv_ln_w, rwkv_ln_b, w_out, ln1_g, ln1_b, xa_wq, xa_wk, xa_wv, xa_wo, ln2_g, ln2_b, mlp_w1, mlp_w2, ln3_g, ln3_b):
    B, T, D = x_prompt.shape
    Bs, Ts, _ = x_sample.shape
    assert w_in.shape[0] == DEPTH

    cfg_p = dict(tm_mm=1024, tm=512, tf=1024, tm_ln=256, lru_bb=2, lru_tc=256, rwkv_bb=B, rwkv_tc=64,
                 att_bb=1, att_tq=256)
    cfg_s = dict(tm_mm=1024, tm=512, tf=1024, tm_ln=256, lru_bb=8, lru_tc=Ts, rwkv_bb=8, rwkv_tc=Ts,
                 att_bb=2, att_tq=Ts)

    yp, ys = x_prompt, x_sample
    outs_p = [[] for _ in range(6)]
    outs_s = [[] for _ in range(4)]
    for l in range(DEPTH):
        row = lambda a: a[l].reshape(1, -1)
        wts = dict(
            w_in_t=jnp.swapaxes(w_in, 1, 2),
            w_in_tail=jnp.pad(jnp.swapaxes(w_in, 1, 2)[l, P_PAD - LORA_PAD:],
                              ((0, P_PAD - P_TOTAL), (0, 0))).astype(BF16),
            conv_w=lru_conv_w[l], conv_b=row(lru_conv_b),
            lru_wa=lru_wa[l].astype(BF16), lru_ba=row(lru_ba),
            lru_wx=lru_wx[l].astype(BF16), lru_bx=row(lru_bx), lru_L=row(lru_L),
            mu=_pad_shift(rwkv_mu[l]).reshape(1, -1), w0=row(rwkv_w0), a0=row(rwkv_a0),
            ww=_lora_rows(rwkv_w_up[l], 0, LORA_WA_PAD), wa=_lora_rows(rwkv_a_up[l], DECAY_LORA, LORA_WA_PAD),
            wg=_lora_rows(rwkv_g_up[l], 0, LORA_G_PAD),
            k_k=row(rwkv_k_k), k_a=row(rwkv_k_a), r_k=row(rwkv_r_k), ln_w=row(rwkv_ln_w), ln_b=row(rwkv_ln_b),
            w_out_a=w_out[l, :LRU_WIDTH].astype(BF16), w_out_b=w_out[l, LRU_WIDTH:].astype(BF16),
            ln1_g=row(ln1_g), ln1_b=row(ln1_b),
            wq=xa_wq, wq_b=xa_wq[l].astype(BF16), wo=xa_wo[l].astype(BF16), ln2_g=row(ln2_g), ln2_b=row(ln2_b),
            w1=mlp_w1[l].astype(BF16), w2=mlp_w2[l].astype(BF16), ln3_g=row(ln3_g), ln3_b=row(ln3_b),
        )
        mem2 = mem_prompt.reshape(B * N_MEM, D)
        mem_k = _matmul(mem2, xa_wk, l, cfg_p["tm_mm"], 512).reshape(B, N_MEM, D)
        mem_v = _matmul(mem2, xa_wv, l, cfg_p["tm_mm"], 512).reshape(B, N_MEM, D)
        yp, hl, cl, Sl, shl = _hybrid_layer(
            yp, mem_k, mem_v,
            jnp.zeros((B, LRU_WIDTH), F32), jnp.zeros((B, CONV_WIDTH - 1, LRU_WIDTH), F32),
            jnp.zeros((1, B, RWKV_HEADS, RWKV_HEAD, RWKV_HEAD), F32), jnp.zeros((B, SHIFT_WIDTH), F32),
            wts, 0, cfg_p)
        for lst, val in zip(outs_p, (mem_k.reshape(B, N_MEM, X_HEADS, X_HEAD_DIM),
                                     mem_v.reshape(B, N_MEM, X_HEADS, X_HEAD_DIM), hl, cl, Sl, shl)):
            lst.append(val)
        ys, hl, cl, Sl, shl = _hybrid_layer(
            ys, cache_mem_k[l], cache_mem_v[l],
            state_lru_h[l], state_lru_conv[l], state_rwkv_S, state_rwkv_shift[l], wts, l, cfg_s)
        for lst, val in zip(outs_s, (hl, cl, Sl, shl)):
            lst.append(val)

    return (yp, ys, *(jnp.stack(o) for o in outs_p), *(jnp.stack(o) for o in outs_s))
```
